```python
import math
import jax, jax.numpy as jnp
from jax import lax
import numpy as np

D_MODEL = 1024
BATCH = 4
SEQ = 4096
DEPTH = 2
DEC_BATCH = 32
DEC_SEQ = 8
PAST_LEN = 8192
PAGE_SIZE = 128

BRANCH_W = D_MODEL // 2
N_BRANCH = 3
A_HEADS = 4
A_DK = BRANCH_W // A_HEADS
A_DV = BRANCH_W // A_HEADS
A_KEY = A_HEADS * A_DK
A_VAL = A_HEADS * A_DV
A_CHUNK = 64
B_HEADS = 4
B_DH = BRANCH_W // (2 * B_HEADS)
B_QK = B_HEADS * 2 * B_DH
B_VAL = B_HEADS * 2 * B_DH
Q_BLOCK = 128
ROPE_THETA = 10000.0
C_GROUPS = 4
C_WIDTH = BRANCH_W
C_GW = C_WIDTH // C_GROUPS
C_CHUNK = 128
D_FF = ((8 * D_MODEL // 3 + 127) // 128) * 128
CONV_W = 3
SPLIT_SIZES = (A_KEY, A_KEY, A_VAL, A_VAL, B_QK, B_QK, B_VAL, C_WIDTH, C_WIDTH, N_BRANCH * D_MODEL)
IN_COLS = sum(SPLIT_SIZES)
EPS = 1e-6
NEG_INF = -1e30

kernel_name = 'hybrid_hgrn2_diffattn_chunkmlp_step'


def rmsnorm(x, w):
    xf = x.astype(jnp.float32)
    y = xf * lax.rsqrt(jnp.mean(xf * xf, axis=-1, keepdims=True) + EPS) * w.astype(jnp.float32)
    return y.astype(x.dtype)


def layernorm(x, w, b):
    xf = x.astype(jnp.float32)
    xc = xf - jnp.mean(xf, axis=-1, keepdims=True)
    y = xc * lax.rsqrt(jnp.mean(xc * xc, axis=-1, keepdims=True) + EPS)
    return (y * w.astype(jnp.float32) + b.astype(jnp.float32)).astype(x.dtype)


def rope(t, pos):
    half = t.shape[-1] // 2
    inv = ROPE_THETA ** (-jnp.arange(half, dtype=jnp.float32) / half)
    ang = pos.astype(jnp.float32)[:, None] * inv[None, :]
    cos = jnp.cos(ang)[None, :, None, None, :]
    sin = jnp.sin(ang)[None, :, None, None, :]
    tf = t.astype(jnp.float32)
    t1, t2 = tf[..., :half], tf[..., half:]
    return jnp.concatenate([t1 * cos - t2 * sin, t1 * sin + t2 * cos], axis=-1).astype(t.dtype)


def hgrn2_chunked(q, k, v, logf, s0):
    bsz, seq_len, n_heads, _ = q.shape
    dv = v.shape[-1]
    c = min(A_CHUNK, seq_len)
    n_chunks = -(-seq_len // c)
    pad = n_chunks * c - seq_len

    def to_chunks(t):
        t = jnp.pad(t, ((0, 0), (0, pad), (0, 0), (0, 0)))
        return t.reshape(bsz, n_chunks, c, n_heads, t.shape[-1]).transpose(1, 0, 3, 2, 4)

    causal = jnp.tril(jnp.ones((c, c), dtype=bool))[:, :, None]

    def step(s, inp):
        qc, kc, vc, gc = inp
        b = jnp.cumsum(gc, axis=2)
        o_inter = jnp.einsum('bhtd,bhde->bhte', qc * jnp.exp(b), s)
        diff = b[:, :, :, None, :] - b[:, :, None, :, :]
        decay = jnp.exp(jnp.where(causal, diff, NEG_INF))
        scores = jnp.einsum('bhtd,bhsd,bhtsd->bhts', qc, kc, decay)
        o = o_inter + jnp.einsum('bhts,bhse->bhte', scores, vc)
        b_last = b[:, :, -1:, :]
        s_new = (jnp.exp(b_last[:, :, 0, :])[..., None] * s
                 + jnp.einsum('bhsd,bhse->bhde', kc * jnp.exp(b_last - b), vc))
        return s_new, o

    s_fin, o = lax.scan(step, s0, (to_chunks(q), to_chunks(k), to_chunks(v), to_chunks(logf)))
    o = o.transpose(1, 0, 3, 2, 4).reshape(bsz, n_chunks * c, n_heads, dv)[:, :seq_len]
    return o, s_fin


def diff_attention(q, k, v, q_pos, k_pos, lam):
    bsz, lq, n_heads, _, dh = q.shape
    qb = min(Q_BLOCK, lq)
    nb = -(-lq // qb)
    pad = nb * qb - lq
    qf = jnp.pad(q.astype(jnp.float32), ((0, 0), (0, pad), (0, 0), (0, 0), (0, 0)))
    qp = jnp.pad(q_pos, (0, pad), mode='edge')
    q_blocks = qf.reshape(bsz, nb, qb, n_heads, 2, dh).transpose(1, 0, 2, 3, 4, 5)
    p_blocks = qp.reshape(nb, qb)
    kf = k.astype(jnp.float32)
    vf = v.astype(jnp.float32)
    scale = dh ** -0.5

    def one_block(args):
        qblk, pblk = args
        s = jnp.einsum('bqhcd,bkhcd->bhcqk', qblk, kf) * scale
        mask = k_pos[None, :] <= pblk[:, None]
        p = jax.nn.softmax(jnp.where(mask, s, NEG_INF), axis=-1)
        a = p[:, :, 0] - lam * p[:, :, 1]
        return jnp.einsum('bhqk,bkhe->bqhe', a, vf)

    o = lax.map(one_block, (q_blocks, p_blocks))
    return o.transpose(1, 0, 2, 3, 4).reshape(bsz, nb * qb, n_heads, -1)[:, :lq]


def chunk_token_mlp(u, v, w_s, b_s):
    bsz, seq_len, _ = v.shape
    n_chunks = -(-seq_len // C_CHUNK)
    pad = n_chunks * C_CHUNK - seq_len
    vp = jnp.pad(v, ((0, 0), (0, pad), (0, 0))).reshape(bsz, n_chunks, C_CHUNK, C_GROUPS, C_GW)
    tril = jnp.tril(jnp.ones((C_CHUNK, C_CHUNK), dtype=bool))
    wm = jnp.where(tril[None], w_s, 0.0).astype(v.dtype)
    mixed = jnp.einsum('gts,bcsgj->bctgj', wm, vp) + b_s.T.astype(v.dtype)[:, :, None]
    mixed = mixed.reshape(bsz, n_chunks * C_CHUNK, C_WIDTH)[:, :seq_len]
    return u * mixed


def trunk_layer(x, layer_idx, pos0, s0, k_past, v_past, conv_buf, w):
    bsz, seq_len, _ = x.shape
    f32 = jnp.float32
    h = rmsnorm(x, w['norm_mix_w'])
    z = jnp.einsum('bld,dc->blc', h, w['w_in'])
    cuts = np.cumsum(SPLIT_SIZES)[:-1].tolist()
    za_q, za_f, za_i, za_g, zb_q, zb_k, zb_v, zc_u, zc_v, z_gate = jnp.split(z, cuts, axis=-1)

    lb = w['lb'].reshape(A_HEADS, A_DK)
    fz = za_f.astype(f32).reshape(bsz, seq_len, A_HEADS, A_DK)
    sig_f = jax.nn.sigmoid(fz)
    logf = jnp.log(lb + (1.0 - lb) * sig_f)
    ka = (1.0 - lb) * (1.0 - sig_f)
    qa = jax.nn.silu(za_q.astype(f32)).reshape(bsz, seq_len, A_HEADS, A_DK)
    ia = za_i.astype(f32).reshape(bsz, seq_len, A_HEADS, A_DV)
    oa, s_new = hgrn2_chunked(qa, ka, ia, logf, s0.astype(f32))
    oa = rmsnorm(oa, w['a_norm_w']) * jax.nn.sigmoid(za_g.astype(f32).reshape(bsz, seq_len, A_HEADS, A_DV))
    oa = oa.reshape(bsz, seq_len, A_VAL).astype(x.dtype)

    pos_q = pos0 + jnp.arange(seq_len, dtype=jnp.int32)
    qb = rope(rmsnorm(zb_q.reshape(bsz, seq_len, B_HEADS, 2, B_DH), w['q_norm_w']), pos_q)
    kb = rope(rmsnorm(zb_k.reshape(bsz, seq_len, B_HEADS, 2, B_DH), w['k_norm_w']), pos_q)
    vb = zb_v.reshape(bsz, seq_len, B_HEADS, 2 * B_DH)
    k_rows = kb.reshape(bsz, seq_len, B_HEADS, 2 * B_DH)
    if k_past is None:
        k_all, v_all = kb, vb
    else:
        k_all = jnp.concatenate([k_past.reshape(bsz, -1, B_HEADS, 2, B_DH).astype(kb.dtype), kb], axis=1)
        v_all = jnp.concatenate([v_past.astype(vb.dtype), vb], axis=1)
    pos_k = jnp.arange(k_all.shape[1], dtype=jnp.int32)
    lam_init = 0.8 - 0.6 * math.exp(-0.3 * layer_idx)
    lam = (jnp.exp(jnp.sum(w['lambda_q1'].astype(f32) * w['lambda_k1'].astype(f32)))
           - jnp.exp(jnp.sum(w['lambda_q2'].astype(f32) * w['lambda_k2'].astype(f32))) + lam_init)
    ob = diff_attention(qb, k_all, v_all, pos_q, pos_k, lam)
    ob = (rmsnorm(ob, w['b_subln_w']) * (1.0 - lam_init)).reshape(bsz, seq_len, B_VAL).astype(x.dtype)

    uc = jax.nn.gelu(zc_u)
    vc = layernorm(jax.nn.gelu(zc_v), w['c_ln_w'], w['c_ln_b'])
    oc = chunk_token_mlp(uc, vc, w['c_w_s'], w['c_b_s'])

    gates = jax.nn.sigmoid(z_gate.astype(f32)).reshape(bsz, seq_len, N_BRANCH, D_MODEL).astype(x.dtype)
    merged = (gates[:, :, 0] * (oa @ w['w_branch_a'])
              + gates[:, :, 1] * (ob @ w['w_branch_b'])
              + gates[:, :, 2] * (oc @ w['w_branch_c']))
    x = x + merged @ w['w_out']

    h2 = rmsnorm(x, w['norm_ffn_w'])
    a, bb = jnp.split(h2 @ w['w_up'], 2, axis=-1)
    a_full = jnp.concatenate([conv_buf.astype(a.dtype), a], axis=1)
    conv = w['conv_b']
    for j in range(CONV_W):
        conv = conv + a_full[:, j:j + seq_len] * w['conv_w'][j]
    x = x + (jax.nn.gelu(conv) * bb) @ w['w_down']
    new_conv = a_full[:, seq_len:]
    return x, s_new.astype(x.dtype), k_rows, vb, vc, new_conv


def setup_inputs(seed: int = 0) -> dict:
    key = jax.random.key(seed)
    ks = jax.random.split(key, 32)
    f32 = jnp.float32

    def nrm(k, shape, scale):
        return scale * jax.random.normal(k, shape, f32)

    n_pages = PAST_LEN // PAGE_SIZE
    n_used = DEC_BATCH * n_pages
    n_pool = n_used + (n_used + 3) // 4
    page_table = jax.random.permutation(ks[0], n_pool)[:n_used].reshape(DEC_BATCH, n_pages).astype(jnp.int32)
    kv_shape = (DEPTH, n_pool, PAGE_SIZE, B_HEADS, 2 * B_DH)
    return dict(
        x_prompt=nrm(ks[1], (BATCH, SEQ, D_MODEL), 1.0),
        x_sample=nrm(ks[2], (DEC_BATCH, DEC_SEQ, D_MODEL), 1.0),
        cache_k=nrm(ks[3], kv_shape, 1.0),
        cache_v=nrm(ks[4], kv_shape, 1.0),
        page_table=page_table,
        state_hgrn=nrm(ks[5], (DEPTH, DEC_BATCH, A_HEADS, A_DK, A_DV), 0.3),
        state_conv=nrm(ks[6], (DEPTH, DEC_BATCH, CONV_W - 1, D_FF), 1.0),
        norm_mix_w=1.0 + nrm(ks[7], (DEPTH, D_MODEL), 0.01),
        w_in=nrm(ks[8], (DEPTH, D_MODEL, IN_COLS), D_MODEL ** -0.5),
        lb_logits=nrm(ks[9], (DEPTH, A_KEY), 1.0),
        a_norm_w=1.0 + nrm(ks[10], (DEPTH, A_DV), 0.01),
        q_norm_w=1.0 + nrm(ks[11], (DEPTH, B_DH), 0.01),
        k_norm_w=1.0 + nrm(ks[12], (DEPTH, B_DH), 0.01),
        lambda_q1=nrm(ks[13], (DEPTH, B_DH), 0.1),
        lambda_k1=nrm(ks[14], (DEPTH, B_DH), 0.1),
        lambda_q2=nrm(ks[15], (DEPTH, B_DH), 0.1),
        lambda_k2=nrm(ks[16], (DEPTH, B_DH), 0.1),
        b_subln_w=1.0 + nrm(ks[17], (DEPTH, 2 * B_DH), 0.01),
        c_ln_w=1.0 + nrm(ks[18], (DEPTH, C_WIDTH), 0.01),
        c_ln_b=nrm(ks[19], (DEPTH, C_WIDTH), 0.01),
        c_w_s=nrm(ks[20], (DEPTH, C_GROUPS, C_CHUNK, C_CHUNK), 0.5 * C_CHUNK ** -0.5),
        c_b_s=1.0 + nrm(ks[21], (DEPTH, C_GROUPS, C_CHUNK), 0.01),
        w_branch_a=nrm(ks[22], (DEPTH, A_VAL, D_MODEL), A_VAL ** -0.5),
        w_branch_b=nrm(ks[23], (DEPTH, B_VAL, D_MODEL), B_VAL ** -0.5),
        w_branch_c=nrm(ks[24], (DEPTH, C_WIDTH, D_MODEL), C_WIDTH ** -0.5),
        w_out=nrm(ks[25], (DEPTH, D_MODEL, D_MODEL), D_MODEL ** -0.5),
        norm_ffn_w=1.0 + nrm(ks[26], (DEPTH, D_MODEL), 0.01),
        w_up=nrm(ks[27], (DEPTH, D_MODEL, 2 * D_FF), D_MODEL ** -0.5),
        conv_w=nrm(ks[28], (DEPTH, CONV_W, D_FF), CONV_W ** -0.5),
        conv_b=nrm(ks[29], (DEPTH, D_FF), 0.01),
        w_down=nrm(ks[30], (DEPTH, D_FF, D_MODEL), D_FF ** -0.5),
    )


def reference(x_prompt, x_sample, cache_k, cache_v, page_table, state_hgrn, state_conv,
              norm_mix_w, w_in, lb_logits, a_norm_w, q_norm_w, k_norm_w,
              lambda_q1, lambda_k1, lambda_q2, lambda_k2, b_subln_w,
              c_ln_w, c_ln_b, c_w_s, c_b_s, w_branch_a, w_branch_b, w_branch_c, w_out,
              norm_ffn_w, w_up, conv_w, conv_b, w_down):
    p_lb = jax.nn.softmax(lb_logits.astype(jnp.float32), axis=0)
    lower_bounds = jnp.cumsum(p_lb, axis=0) - p_lb[0:1]
    n_seq_d, n_pages = page_table.shape
    past_len = n_pages * PAGE_SIZE
    n_seq_p = x_prompt.shape[0]

    xp, xs = x_prompt, x_sample
    kp_l, vp_l, hp_l, cp_l = [], [], [], []
    ks_l, vs_l, hs_l, cs_l, chs_l = [], [], [], [], []
    for l in range(DEPTH):
        w = dict(norm_mix_w=norm_mix_w[l], w_in=w_in[l], lb=lower_bounds[l], a_norm_w=a_norm_w[l],
                 q_norm_w=q_norm_w[l], k_norm_w=k_norm_w[l],
                 lambda_q1=lambda_q1[l], lambda_k1=lambda_k1[l],
                 lambda_q2=lambda_q2[l], lambda_k2=lambda_k2[l], b_subln_w=b_subln_w[l],
                 c_ln_w=c_ln_w[l], c_ln_b=c_ln_b[l], c_w_s=c_w_s[l], c_b_s=c_b_s[l],
                 w_branch_a=w_branch_a[l], w_branch_b=w_branch_b[l], w_branch_c=w_branch_c[l],
                 w_out=w_out[l], norm_ffn_w=norm_ffn_w[l], w_up=w_up[l],
                 conv_w=conv_w[l], conv_b=conv_b[l], w_down=w_down[l])
        s0_p = jnp.zeros((n_seq_p, A_HEADS, A_DK, A_DV), jnp.float32)
        conv0_p = jnp.zeros((n_seq_p, CONV_W - 1, D_FF), xp.dtype)
        xp, s_p, k_p, v_p, _, c_p = trunk_layer(xp, l, 0, s0_p, None, None, conv0_p, w)
        kp_l.append(k_p); vp_l.append(v_p); hp_l.append(s_p); cp_l.append(c_p)
        k_past = cache_k[l][page_table].reshape(n_seq_d, past_len, B_HEADS, 2 * B_DH)
        v_past = cache_v[l][page_table].reshape(n_seq_d, past_len, B_HEADS, 2 * B_DH)
        xs, s_s, k_s, v_s, ch_s, c_s = trunk_layer(xs, l, past_len, state_hgrn[l], k_past, v_past,
                                                 state_conv[l], w)
        ks_l.append(k_s); vs_l.append(v_s); hs_l.append(s_s); cs_l.append(c_s); chs_l.append(ch_s)

    new_k_prompt = jnp.stack(kp_l)
    new_v_prompt = jnp.stack(vp_l)
    new_hgrn_prompt = jnp.stack(hp_l)
    new_conv_prompt = jnp.stack(cp_l)
    new_k_sample = jnp.stack(ks_l)
    new_v_sample = jnp.stack(vs_l)
    new_hgrn_sample = jnp.stack(hs_l)
    new_conv_sample = jnp.stack(cs_l)
    new_chunk_v_sample = jnp.stack(chs_l)
    return (xp, xs, new_k_prompt, new_v_prompt, new_hgrn_prompt, new_conv_prompt,
            new_k_sample, new_v_sample, new_hgrn_sample, new_conv_sample, new_chunk_v_sample)
```

```python
import functools
import math

import jax
import jax.numpy as jnp
from jax import lax
from jax.experimental import pallas as pl
from jax.experimental.pallas import tpu as pltpu

F32 = jnp.float32
BF16 = jnp.bfloat16

D_MODEL = 1024
BRANCH_W = 512
N_HEADS = 4
HEAD_W = 128
QK_DH = 64
ROPE_THETA = 10000.0
A_CHUNK = 64
C_CHUNK = 128
PAGE_SIZE = 128
D_FF = 2816
CONV_W = 3
IN_COLS = 9 * BRANCH_W + 3 * D_MODEL
EPS = 1e-6
NEG_INF = -1e30
LANES = 128
SUBLANES = 8

COL_AQ, COL_AF, COL_AI, COL_AG, COL_BQ, COL_BK, COL_BV, COL_CU, COL_CV = range(9)
GATE_BLOCK_W = 1536


def _sigmoid(x):
    return 1.0 / (1.0 + jnp.exp(-x))


def _gelu(x):
    c = math.sqrt(2.0 / math.pi)
    return 0.5 * x * (1.0 + jnp.tanh(c * (x + 0.044715 * (x * x * x))))


def _dot(a, b):
    return jnp.dot(a, b, preferred_element_type=F32)


def _dot_nt(a, b):
    return lax.dot_general(a, b, (((1,), (1,)), ((), ())), preferred_element_type=F32)


def _dot_tn(a, b):
    return lax.dot_general(a, b, (((0,), (0,)), ((), ())), preferred_element_type=F32)


def _cumsum_rows(x):
    row = lax.broadcasted_iota(jnp.int32, x.shape, 0)
    d = 1
    while d < x.shape[0]:
        x = x + jnp.where(row >= d, pltpu.roll(x, d, 0), 0.0)
        d *= 2
    return x


def _pad_rows(a, rows):
    if a.shape[0] >= rows:
        return a
    return jnp.concatenate([a, jnp.zeros((rows - a.shape[0],) + a.shape[1:], a.dtype)], axis=0)


def _log2(n):
    assert n & (n - 1) == 0, n
    return n.bit_length() - 1


def _inproj_kernel(x_ref, nw_ref, w_ref, z_ref, h_ref):
    @pl.when(pl.program_id(1) == 0)
    def _():
        x = x_ref[...]
        ms = jnp.mean(x * x, axis=-1, keepdims=True)
        h_ref[...] = (x * lax.rsqrt(ms + EPS) * nw_ref[...]).astype(BF16)

    z_ref[...] = _dot(h_ref[...], w_ref[...])


def _inproj(x, norm_w, w_bf16, tm, tn):
    t = x.shape[0]
    return pl.pallas_call(
        _inproj_kernel,
        grid=(t // tm, IN_COLS // tn),
        in_specs=[
            pl.BlockSpec((tm, D_MODEL), lambda i, j: (i, 0)),
            pl.BlockSpec((1, D_MODEL), lambda i, j: (0, 0)),
            pl.BlockSpec((D_MODEL, tn), lambda i, j: (0, j)),
        ],
        out_specs=pl.BlockSpec((tm, tn), lambda i, j: (i, j)),
        out_shape=jax.ShapeDtypeStruct((t, IN_COLS), F32),
        scratch_shapes=[pltpu.VMEM((tm, D_MODEL), BF16)],
        compiler_params=pltpu.CompilerParams(dimension_semantics=("parallel", "arbitrary")),
        name="inproj",
    )(x, norm_w.reshape(1, D_MODEL), w_bf16)


def _hgrn_kernel(zq_ref, zf_ref, zi_ref, zg_ref, lb_ref, nw_ref, s0_ref, oa_ref, sfin_ref,
                 st_ref, k_s, b_s, *, chunk):
    c = pl.program_id(1)
    nc = pl.num_programs(1)

    @pl.when(c == 0)
    def _():
        for h in range(N_HEADS):
            st_ref[h] = s0_ref[0, h].T

    t_idx = lax.broadcasted_iota(jnp.int32, (chunk, HEAD_W), 0)
    lane = lax.broadcasted_iota(jnp.int32, (chunk, LANES), 1)
    mm_rows = max(chunk, 2 * SUBLANES)

    for h in range(N_HEADS):
        sl = slice(h * HEAD_W, (h + 1) * HEAD_W)
        lb = lb_ref[:, sl]
        sig = _sigmoid(zf_ref[:, sl])
        logf = jnp.log(lb + (1.0 - lb) * sig)
        k = (1.0 - lb) * (1.0 - sig)
        zq = zq_ref[:, sl]
        q = zq * _sigmoid(zq)
        v = zi_ref[:, sl]

        b = _cumsum_rows(logf)
        k_s[...] = k
        b_s[...] = b

        def body(s, sc, q=q, b=b):
            ks = k_s[pl.ds(s, 1), :]
            bs = b_s[pl.ds(s, 1), :]
            dec = jnp.exp(jnp.where(t_idx >= s, b - bs, NEG_INF))
            colv = jnp.sum(q * ks * dec, axis=-1, keepdims=True)
            return jnp.where(lane == s, colv, sc)

        scores = lax.fori_loop(0, chunk, body, jnp.zeros((chunk, LANES), F32))

        st = st_ref[h]
        v_pad = _pad_rows(v, LANES).astype(BF16)
        qd = _pad_rows(q * jnp.exp(b), mm_rows).astype(BF16)
        sc_bf = _pad_rows(scores, mm_rows).astype(BF16)
        o = (_dot_nt(qd, st.astype(BF16)) + _dot(sc_bf, v_pad))[:chunk]

        b_last = b[chunk - 1:chunk, :]
        kd = _pad_rows(k * jnp.exp(b_last - b), LANES).astype(BF16)
        st_ref[h] = jnp.exp(b_last) * st + _dot_tn(v_pad, kd)

        ms = jnp.mean(o * o, axis=-1, keepdims=True)
        on = o * lax.rsqrt(ms + EPS) * nw_ref[...]
        oa_ref[:, sl] = (on * _sigmoid(zg_ref[:, sl])).astype(oa_ref.dtype)

    @pl.when(c == nc - 1)
    def _():
        for h in range(N_HEADS):
            sfin_ref[0, h] = st_ref[h].T


def _hgrn(z, lb, norm_w, s0, nseq, seq_len, chunk, out_dtype):
    nc = seq_len // chunk
    zspec = lambda cb: pl.BlockSpec((chunk, BRANCH_W), lambda b, c: (b * nc + c, cb))
    return pl.pallas_call(
        functools.partial(_hgrn_kernel, chunk=chunk),
        grid=(nseq, nc),
        in_specs=[
            zspec(COL_AQ), zspec(COL_AF), zspec(COL_AI), zspec(COL_AG),
            pl.BlockSpec((1, BRANCH_W), lambda b, c: (0, 0)),
            pl.BlockSpec((1, HEAD_W), lambda b, c: (0, 0)),
            pl.BlockSpec((1, N_HEADS, HEAD_W, HEAD_W), lambda b, c: (b, 0, 0, 0)),
        ],
        out_specs=[
            pl.BlockSpec((chunk, BRANCH_W), lambda b, c: (b * nc + c, 0)),
            pl.BlockSpec((1, N_HEADS, HEAD_W, HEAD_W), lambda b, c: (b, 0, 0, 0)),
        ],
        out_shape=[
            jax.ShapeDtypeStruct((nseq * seq_len, BRANCH_W), out_dtype),
            jax.ShapeDtypeStruct((nseq, N_HEADS, HEAD_W, HEAD_W), F32),
        ],
        scratch_shapes=[
            pltpu.VMEM((N_HEADS, HEAD_W, HEAD_W), F32),
            pltpu.VMEM((chunk, HEAD_W), F32),
            pltpu.VMEM((chunk, HEAD_W), F32),
        ],
        compiler_params=pltpu.CompilerParams(dimension_semantics=("parallel", "arbitrary")),
        name="hgrn",
    )(z, z, z, z, lb.reshape(1, BRANCH_W), norm_w.reshape(1, HEAD_W), s0)


def _qkv_kernel(zq_ref, zk_ref, zv_ref, cos_ref, sa_ref, sb_ref, qw_ref, kw_ref, g_ref,
                q_out, kf_out, vf_out, *bf_outs):
    cos = cos_ref[...]
    sa = sa_ref[...]
    sb = sb_ref[...]
    gmat = g_ref[...]

    def norm_rope(t, w):
        sq = t * t
        hi = sq.astype(BF16)
        lo = (sq - hi.astype(F32)).astype(BF16)
        ss = _dot(hi, gmat) + _dot(lo, gmat)
        y = t * lax.rsqrt(ss * (1.0 / QK_DH) + EPS) * w
        return (y * cos + pltpu.roll(y, LANES - QK_DH // 2, 1) * sa
                + pltpu.roll(y, QK_DH // 2, 1) * sb)

    for h in range(N_HEADS):
        sl = slice(h * HEAD_W, (h + 1) * HEAD_W)
        qr = norm_rope(zq_ref[:, sl], qw_ref[...])
        kr = norm_rope(zk_ref[:, sl], kw_ref[...])
        q_out[:, sl] = (qr * (QK_DH ** -0.5)).astype(q_out.dtype)
        kf_out[:, sl] = kr
        v = zv_ref[:, sl]
        vf_out[:, sl] = v
        if bf_outs:
            bf_outs[0][:, sl] = kr.astype(BF16)
            bf_outs[1][:, sl] = v.astype(BF16)


def _qkv_prep(z, tabs, q_norm_w, k_norm_w, tm, tab_blocks, emit_bf16):
    t = z.shape[0]
    zspec = lambda cb: pl.BlockSpec((tm, BRANCH_W), lambda i: (i, cb))
    tspec = pl.BlockSpec((tm, LANES), lambda i: (i % tab_blocks, 0))
    wspec = pl.BlockSpec((1, LANES), lambda i: (0, 0))
    ospec = pl.BlockSpec((tm, BRANCH_W), lambda i: (i, 0))
    gi = lax.broadcasted_iota(jnp.int32, (LANES, LANES), 0) // QK_DH
    gj = lax.broadcasted_iota(jnp.int32, (LANES, LANES), 1) // QK_DH
    gmat = (gi == gj).astype(BF16)
    n_out = 5 if emit_bf16 else 3
    dts = [BF16 if emit_bf16 else F32, F32, F32, BF16, BF16][:n_out]
    return pl.pallas_call(
        _qkv_kernel,
        grid=(t // tm,),
        in_specs=[zspec(COL_BQ), zspec(COL_BK), zspec(COL_BV), tspec, tspec, tspec, wspec, wspec,
                  pl.BlockSpec((LANES, LANES), lambda i: (0, 0))],
        out_specs=[ospec] * n_out,
        out_shape=[jax.ShapeDtypeStruct((t, BRANCH_W), d) for d in dts],
        compiler_params=pltpu.CompilerParams(dimension_semantics=("parallel",)),
        name="qkv_prep",
    )(z, z, z, *tabs, jnp.tile(q_norm_w, 2).reshape(1, LANES),
      jnp.tile(k_norm_w, 2).reshape(1, LANES), gmat)


def _rope_tables(pos0, seq_len, rows):
    half = QK_DH // 2
    inv = ROPE_THETA ** (-jnp.arange(half, dtype=F32) / half)
    pos = (pos0 + jnp.arange(seq_len, dtype=jnp.int32)).astype(F32)
    ang = pos[:, None] * inv[None, :]
    cos, sin = jnp.cos(ang), jnp.sin(ang)
    zero = jnp.zeros_like(sin)
    per_head = lambda a, b: jnp.tile(jnp.concatenate([a, b], axis=-1), (rows // seq_len, LANES // QK_DH))
    return per_head(cos, cos), per_head(-sin, zero), per_head(zero, sin)


def _softmax_step(s, v_bf16, m_ref, l_ref, acc_ref):
    m_old = m_ref[...]
    m_new = jnp.maximum(m_old, jnp.max(s, axis=-1, keepdims=True))
    alpha = jnp.exp(m_old - m_new)
    p = jnp.exp(s - m_new)
    l_ref[...] = alpha * l_ref[...] + jnp.sum(p, axis=-1, keepdims=True)
    acc_ref[...] = alpha * acc_ref[...] + _dot(p.astype(BF16), v_bf16)
    m_ref[...] = m_new


def _subln(o, w, scale):
    ms = jnp.mean(o * o, axis=-1, keepdims=True)
    return o * lax.rsqrt(ms + EPS) * w * scale


def _attn_kernel(lam_ref, w_ref, q_ref, k_ref, v_ref, o_ref, m1, l1, a1, m2, l2, a2,
                 *, tq, out_scale):
    qi = pl.program_id(2)
    ki = pl.program_id(3)
    nk = pl.num_programs(3)

    @pl.when(ki == 0)
    def _():
        for m, l, a in ((m1, l1, a1), (m2, l2, a2)):
            m[...] = jnp.full(m.shape, NEG_INF, F32)
            l[...] = jnp.zeros(l.shape, F32)
            a[...] = jnp.zeros(a.shape, F32)

    def update(masked):
        q = q_ref[...]
        k = k_ref[...]
        v = v_ref[...]
        lane = lax.broadcasted_iota(jnp.int32, q.shape, 1)
        zero = jnp.zeros_like(q)
        s1 = _dot_nt(jnp.where(lane < QK_DH, q, zero), k)
        s2 = _dot_nt(jnp.where(lane >= QK_DH, q, zero), k)
        if masked:
            r = lax.broadcasted_iota(jnp.int32, s1.shape, 0)
            c = lax.broadcasted_iota(jnp.int32, s1.shape, 1)
            s1 = jnp.where(c <= r, s1, NEG_INF)
            s2 = jnp.where(c <= r, s2, NEG_INF)
        _softmax_step(s1, v, m1, l1, a1)
        _softmax_step(s2, v, m2, l2, a2)

    @pl.when(ki < qi)
    def _():
        update(False)

    @pl.when(ki == qi)
    def _():
        update(True)

    @pl.when(ki == nk - 1)
    def _():
        o = a1[...] / l1[...] - lam_ref[...] * (a2[...] / l2[...])
        o_ref[...] = _subln(o, w_ref[...], out_scale).astype(o_ref.dtype)


def _prompt_attn(q, k, v, lam, subln_w, nseq, seq_len, tq, out_scale, out_dtype):
    nq = seq_len // tq
    qspec = pl.BlockSpec((tq, HEAD_W), lambda b, h, i, j: (b * nq + i, h))
    kspec = pl.BlockSpec((tq, HEAD_W), lambda b, h, i, j: (b * nq + jnp.minimum(i, j), h))
    cspec = pl.BlockSpec((1, HEAD_W), lambda b, h, i, j: (0, 0))
    stat = pltpu.VMEM((tq, 1), F32)
    acc = pltpu.VMEM((tq, HEAD_W), F32)
    return pl.pallas_call(
        functools.partial(_attn_kernel, tq=tq, out_scale=out_scale),
        grid=(nseq, N_HEADS, nq, nq),
        in_specs=[cspec, cspec, qspec, kspec, kspec],
        out_specs=qspec,
        out_shape=jax.ShapeDtypeStruct((nseq * seq_len, BRANCH_W), out_dtype),
        scratch_shapes=[stat, stat, acc, stat, stat, acc],
        compiler_params=pltpu.CompilerParams(
            dimension_semantics=("parallel", "parallel", "parallel", "arbitrary")),
        name="prompt_attn",
    )(jnp.full((1, HEAD_W), lam, F32), subln_w.reshape(1, HEAD_W), q, k, v)


def _sattn_kernel(pt_ref, lam_ref, w_ref, q_ref, kn_ref, vn_ref, *rest, pages, dec_seq, out_scale):
    del pt_ref
    k_refs = rest[:pages]
    v_refs = rest[pages:2 * pages]
    o_ref = rest[2 * pages]
    qall, m_ref, l_ref, acc_ref = rest[2 * pages + 1:]
    g = pl.program_id(1)
    ng = pl.num_programs(1)
    rows = 2 * N_HEADS * dec_seq

    @pl.when(g == 0)
    def _():
        qt = jnp.concatenate([q_ref[...]] * (2 * N_HEADS), axis=0)
        r = lax.broadcasted_iota(jnp.int32, qt.shape, 0) >> _log2(dec_seq)
        c = lax.broadcasted_iota(jnp.int32, qt.shape, 1) >> _log2(QK_DH)
        qall[...] = jnp.where(r == c, qt, 0.0).astype(BF16)
        m_ref[...] = jnp.full(m_ref.shape, NEG_INF, F32)
        l_ref[...] = jnp.zeros(l_ref.shape, F32)
        acc_ref[...] = jnp.zeros(acc_ref.shape, F32)

    kcat = jnp.concatenate([kr[...].astype(BF16) for kr in k_refs], axis=0)
    vcat = jnp.concatenate([vr[...].astype(BF16) for vr in v_refs], axis=0)
    _softmax_step(_dot_nt(qall[...], kcat), vcat, m_ref, l_ref, acc_ref)

    @pl.when(g == ng - 1)
    def _():
        zpad = jnp.zeros((LANES - dec_seq, BRANCH_W), F32)
        kn = jnp.concatenate([kn_ref[...], zpad], axis=0).astype(BF16)
        vn = jnp.concatenate([vn_ref[...], zpad], axis=0).astype(BF16)
        s = _dot_nt(qall[...], kn)
        tok = lax.broadcasted_iota(jnp.int32, s.shape, 0) & (dec_seq - 1)
        key = lax.broadcasted_iota(jnp.int32, s.shape, 1)
        s = jnp.where(key <= tok, s, NEG_INF)
        _softmax_step(s, vn, m_ref, l_ref, acc_ref)
        acc = acc_ref[...]
        l = l_ref[...]
        for h in range(N_HEADS):
            r1 = slice(h * 2 * dec_seq, h * 2 * dec_seq + dec_seq)
            r2 = slice(h * 2 * dec_seq + dec_seq, (h + 1) * 2 * dec_seq)
            sl = slice(h * HEAD_W, (h + 1) * HEAD_W)
            o = acc[r1, sl] / l[r1] - lam_ref[...] * (acc[r2, sl] / l[r2])
            o_ref[:, sl] = _subln(o, w_ref[...], out_scale).astype(o_ref.dtype)


def _sample_attn(q, k_new, v_new, cache_k, cache_v, page_table, layer, lam, subln_w, nseq, dec_seq,
                 pages, out_scale):
    n_pages = page_table.shape[1]
    depth, n_pool = cache_k.shape[:2]
    ck = cache_k.reshape(depth, n_pool, PAGE_SIZE, BRANCH_W)
    cv = cache_v.reshape(depth, n_pool, PAGE_SIZE, BRANCH_W)
    rows = 2 * N_HEADS * dec_seq

    def page_spec(i):
        return pl.BlockSpec((None, None, PAGE_SIZE, BRANCH_W),
                            lambda b, g, pt: (layer, pt[b, g * pages + i], 0, 0))

    cspec = pl.BlockSpec((1, HEAD_W), lambda b, g, pt: (0, 0))
    rspec = pl.BlockSpec((dec_seq, BRANCH_W), lambda b, g, pt: (b, 0))
    grid_spec = pltpu.PrefetchScalarGridSpec(
        num_scalar_prefetch=1,
        grid=(nseq, n_pages // pages),
        in_specs=[cspec, cspec, rspec, rspec, rspec]
        + [page_spec(i) for i in range(pages)] * 2,
        out_specs=rspec,
        scratch_shapes=[
            pltpu.VMEM((rows, BRANCH_W), BF16),
            pltpu.VMEM((rows, 1), F32),
            pltpu.VMEM((rows, 1), F32),
            pltpu.VMEM((rows, BRANCH_W), F32),
        ],
    )
    return pl.pallas_call(
        functools.partial(_sattn_kernel, pages=pages, dec_seq=dec_seq, out_scale=out_scale),
        grid_spec=grid_spec,
        out_shape=jax.ShapeDtypeStruct((nseq * dec_seq, BRANCH_W), F32),
        compiler_params=pltpu.CompilerParams(dimension_semantics=("parallel", "arbitrary")),
        name="sample_attn",
    )(page_table, jnp.full((1, HEAD_W), lam, F32), subln_w.reshape(1, HEAD_W), q, k_new, v_new,
      *([ck] * pages), *([cv] * pages))


def _tmlp_kernel(zu_ref, zv_ref, lnw_ref, lnb_ref, wm_ref, bias_ref, oc_ref, *vc_out,
                 mix_rows, period, n_sub):
    u = _gelu(zu_ref[...])
    g = _gelu(zv_ref[...])
    xc = g - jnp.mean(g, axis=-1, keepdims=True)
    vc = xc * lax.rsqrt(jnp.mean(xc * xc, axis=-1, keepdims=True) + EPS) * lnw_ref[...] + lnb_ref[...]
    if vc_out:
        vc_out[0][...] = vc
    row = lax.broadcasted_iota(jnp.int32, (mix_rows, mix_rows), 0)
    col = lax.broadcasted_iota(jnp.int32, (mix_rows, mix_rows), 1)
    sh = _log2(period)
    allow = ((row >> sh) == (col >> sh)) & ((col & (period - 1)) <= (row & (period - 1)))
    for gi in range(N_HEADS):
        sl = slice(gi * HEAD_W, (gi + 1) * HEAD_W)
        wm = jnp.where(allow, wm_ref[gi], 0.0).astype(BF16)
        for sb in range(n_sub):
            rs = slice(sb * mix_rows, (sb + 1) * mix_rows)
            mixed = _dot(wm, vc[rs, sl].astype(BF16)) + bias_ref[:, sl]
            oc_ref[rs, sl] = (u[rs, sl] * mixed).astype(oc_ref.dtype)


def _token_mlp(z, ln_w, ln_b, w_s, b_s, tile_rows, mix_rows, period, emit_v, out_dtype):
    t = z.shape[0]
    reps = mix_rows // period
    wm = jnp.tile(w_s[:, :period, :period], (1, reps, reps))
    bias = jnp.tile(jnp.repeat(b_s[:, :period].T, HEAD_W, axis=1), (reps, 1))
    zspec = lambda cb: pl.BlockSpec((tile_rows, BRANCH_W), lambda i: (i, cb))
    vec = pl.BlockSpec((1, BRANCH_W), lambda i: (0, 0))
    ospec = pl.BlockSpec((tile_rows, BRANCH_W), lambda i: (i, 0))
    out_shape = [jax.ShapeDtypeStruct((t, BRANCH_W), out_dtype)]
    if emit_v:
        out_shape.append(jax.ShapeDtypeStruct((t, BRANCH_W), F32))
    return pl.pallas_call(
        functools.partial(_tmlp_kernel, mix_rows=mix_rows, period=period, n_sub=tile_rows // mix_rows),
        grid=(t // tile_rows,),
        in_specs=[zspec(COL_CU), zspec(COL_CV), vec, vec,
                  pl.BlockSpec((N_HEADS, mix_rows, mix_rows), lambda i: (0, 0, 0)),
                  pl.BlockSpec((mix_rows, BRANCH_W), lambda i: (0, 0))],
        out_specs=[ospec] * len(out_shape),
        out_shape=out_shape,
        compiler_params=pltpu.CompilerParams(dimension_semantics=("parallel",)),
        name="token_mlp",
    )(z, z, ln_w.reshape(1, BRANCH_W), ln_b.reshape(1, BRANCH_W), wm, bias)


def _merge_kernel(oa_ref, ob_ref, oc_ref, zg0_ref, zg1_ref, x_ref, wa_ref, wb_ref, wc_ref, wo_ref,
                  xo_ref):
    pa = _dot(oa_ref[...].astype(BF16), wa_ref[...])
    pb = _dot(ob_ref[...].astype(BF16), wb_ref[...])
    pc = _dot(oc_ref[...].astype(BF16), wc_ref[...])
    zg0 = zg0_ref[...]
    zg1 = zg1_ref[...]
    split = 2 * D_MODEL - GATE_BLOCK_W
    g_a = _sigmoid(zg0[:, :D_MODEL])
    g_b = _sigmoid(jnp.concatenate([zg0[:, D_MODEL:], zg1[:, :split]], axis=1))
    g_c = _sigmoid(zg1[:, split:])
    merged = g_a * pa + g_b * pb + g_c * pc
    xo_ref[...] = x_ref[...] + _dot(merged.astype(BF16), wo_ref[...])


def _merge(oa, ob, oc, z, x, wa, wb, wc, wo, tm):
    t = x.shape[0]
    bspec = pl.BlockSpec((tm, BRANCH_W), lambda i: (i, 0))
    gspec = lambda cb: pl.BlockSpec((tm, GATE_BLOCK_W), lambda i: (i, cb))
    xspec = pl.BlockSpec((tm, D_MODEL), lambda i: (i, 0))
    wspec = pl.BlockSpec((BRANCH_W, D_MODEL), lambda i: (0, 0))
    return pl.pallas_call(
        _merge_kernel,
        grid=(t // tm,),
        in_specs=[bspec, bspec, bspec, gspec(3), gspec(4), xspec, wspec, wspec, wspec,
                  pl.BlockSpec((D_MODEL, D_MODEL), lambda i: (0, 0))],
        out_specs=xspec,
        out_shape=jax.ShapeDtypeStruct((t, D_MODEL), F32),
        compiler_params=pltpu.CompilerParams(dimension_semantics=("parallel",)),
        name="merge",
    )(oa, ob, oc, z, z, x, wa, wb, wc, wo)


def _ffn_kernel(x_ref, nw_ref, wa_ref, wb_ref, wd_ref, cw_ref, cb_ref, cbuf_ref, xo_ref, last_ref,
                h_s, acc_s, abuf, *, tm, shift, tiles_per_seq, pad):
    i = pl.program_id(0)
    j = pl.program_id(1)
    nff = pl.num_programs(1)
    halo = (CONV_W - 1) * shift

    @pl.when(j == 0)
    def _():
        x = x_ref[...]
        ms = jnp.mean(x * x, axis=-1, keepdims=True)
        h_s[...] = (x * lax.rsqrt(ms + EPS) * nw_ref[...]).astype(BF16)
        acc_s[...] = jnp.zeros(acc_s.shape, F32)

    @pl.when(i % tiles_per_seq == 0)
    def _():
        abuf[j, pad - halo:pad, :] = cbuf_ref[0]

    @pl.when(i % tiles_per_seq != 0)
    def _():
        abuf[j, pad - halo:pad, :] = abuf[j, pad + tm - halo:pad + tm, :]

    h = h_s[...]
    a = _dot(h, wa_ref[...])
    bb = _dot(h, wb_ref[...])
    abuf[j, pad:pad + tm, :] = a
    a1 = abuf[j, pad - shift:pad - shift + tm, :]
    a2 = abuf[j, pad - 2 * shift:pad - 2 * shift + tm, :]
    conv = cb_ref[...] + a2 * cw_ref[0:1, :] + a1 * cw_ref[1:2, :] + a * cw_ref[2:3, :]
    acc_s[...] += _dot((_gelu(conv) * bb).astype(BF16), wd_ref[...])
    last_ref[0] = a[tm - halo:, :]

    @pl.when(j == nff - 1)
    def _():
        xo_ref[...] = x_ref[...] + acc_s[...]


def _ffn(x, norm_w, w_up, w_down, conv_w, conv_b, conv_buf, tm, tff, shift, tiles_per_seq):
    t = x.shape[0]
    nff = D_FF // tff
    halo = (CONV_W - 1) * shift
    pad = -(-halo // SUBLANES) * SUBLANES
    n_tiles = t // tm
    return pl.pallas_call(
        functools.partial(_ffn_kernel, tm=tm, shift=shift, tiles_per_seq=tiles_per_seq, pad=pad),
        grid=(n_tiles, nff),
        in_specs=[
            pl.BlockSpec((tm, D_MODEL), lambda i, j: (i, 0)),
            pl.BlockSpec((1, D_MODEL), lambda i, j: (0, 0)),
            pl.BlockSpec((D_MODEL, tff), lambda i, j: (0, j)),
            pl.BlockSpec((D_MODEL, tff), lambda i, j: (0, nff + j)),
            pl.BlockSpec((tff, D_MODEL), lambda i, j: (j, 0)),
            pl.BlockSpec((CONV_W, tff), lambda i, j: (0, j)),
            pl.BlockSpec((1, tff), lambda i, j: (0, j)),
            pl.BlockSpec((1, halo, tff), lambda i, j: (i // tiles_per_seq, 0, j)),
        ],
        out_specs=[
            pl.BlockSpec((tm, D_MODEL), lambda i, j: (i, 0)),
            pl.BlockSpec((1, halo, tff), lambda i, j: (i, 0, j)),
        ],
        out_shape=[
            jax.ShapeDtypeStruct((t, D_MODEL), F32),
            jax.ShapeDtypeStruct((n_tiles, halo, D_FF), F32),
        ],
        scratch_shapes=[
            pltpu.VMEM((tm, D_MODEL), BF16),
            pltpu.VMEM((tm, D_MODEL), F32),
            pltpu.VMEM((nff, pad + tm, tff), F32),
        ],
        compiler_params=pltpu.CompilerParams(dimension_semantics=("arbitrary", "arbitrary")),
        name="conv_ffn",
    )(x, norm_w.reshape(1, D_MODEL), w_up, w_up, w_down, conv_w, conv_b.reshape(1, D_FF), conv_buf)


def _row_tile(t, want):
    return min(t, want)


def _layer(x, nseq, seq_len, layer, pos0, s0, conv_state, past, w, is_sample):
    t = nseq * seq_len
    tm = _row_tile(t, 512)
    act_dtype = F32 if is_sample else BF16
    z = _inproj(x, w["norm_mix_w"], w["w_in"], tm, 1920)

    chunk = min(A_CHUNK, seq_len)
    oa, s_new = _hgrn(z, w["lb"], w["a_norm_w"], s0, nseq, seq_len, chunk, act_dtype)

    tab_rows = max(seq_len, tm)
    tabs = _rope_tables(pos0, seq_len, tab_rows)
    qkv = _qkv_prep(z, tabs, w["q_norm_w"], w["k_norm_w"], tm, tab_rows // tm, not is_sample)
    lam_init = 0.8 - 0.6 * math.exp(-0.3 * layer)
    lam = (jnp.exp(jnp.sum(w["lambda_q1"] * w["lambda_k1"]))
           - jnp.exp(jnp.sum(w["lambda_q2"] * w["lambda_k2"])) + lam_init)
    if is_sample:
        q, k_rows, v_rows = qkv
        cache_k, cache_v, page_table = past
        ob = _sample_attn(q, k_rows, v_rows, cache_k, cache_v, page_table, layer, lam, w["b_subln_w"],
                          nseq, seq_len, 8, 1.0 - lam_init)
    else:
        q, k_rows, v_rows, k_bf, v_bf = qkv
        ob = _prompt_attn(q, k_bf, v_bf, lam, w["b_subln_w"], nseq, seq_len, min(seq_len, 512),
                          1.0 - lam_init, act_dtype)

    if is_sample:
        oc, vc = _token_mlp(z, w["c_ln_w"], w["c_ln_b"], w["c_w_s"], w["c_b_s"], t, t, seq_len, True,
                            act_dtype)
    else:
        (oc,) = _token_mlp(z, w["c_ln_w"], w["c_ln_b"], w["c_w_s"], w["c_b_s"], tm, C_CHUNK, C_CHUNK,
                           False, act_dtype)
        vc = None

    x_mid = _merge(oa, ob, oc, z, x, w["w_branch_a"], w["w_branch_b"], w["w_branch_c"], w["w_out"], tm)

    if is_sample:
        xt = x_mid.reshape(nseq, seq_len, D_MODEL).transpose(1, 0, 2).reshape(t, D_MODEL)
        cbuf = conv_state.transpose(1, 0, 2).reshape(1, (CONV_W - 1) * nseq, D_FF)
        xo, last = _ffn(xt, w["norm_ffn_w"], w["w_up"], w["w_down"], w["conv_w"], w["conv_b"], cbuf,
                        t, 1408, nseq, 1)
        x_new = xo.reshape(seq_len, nseq, D_MODEL).transpose(1, 0, 2).reshape(t, D_MODEL)
        new_conv = last.reshape(CONV_W - 1, nseq, D_FF).transpose(1, 0, 2)
    else:
        tiles_per_seq = seq_len // tm
        x_new, last = _ffn(x_mid, w["norm_ffn_w"], w["w_up"], w["w_down"], w["conv_w"], w["conv_b"],
                           conv_state, tm, 1408, 1, tiles_per_seq)
        new_conv = last.reshape(nseq, tiles_per_seq, CONV_W - 1, D_FF)[:, -1]
    return x_new, s_new, k_rows, v_rows, vc, new_conv


def kernel(x_prompt, x_sample, cache_k, cache_v, page_table, state_hgrn, state_conv,
           norm_mix_w, w_in, lb_logits, a_norm_w, q_norm_w, k_norm_w,
           lambda_q1, lambda_k1, lambda_q2, lambda_k2, b_subln_w,
           c_ln_w, c_ln_b, c_w_s, c_b_s, w_branch_a, w_branch_b, w_branch_c, w_out,
           norm_ffn_w, w_up, conv_w, conv_b, w_down):
    depth = w_in.shape[0]
    n_p, len_p, _ = x_prompt.shape
    n_s, len_s, _ = x_sample.shape
    past_len = page_table.shape[1] * PAGE_SIZE
    p_lb = jax.nn.softmax(lb_logits.astype(F32), axis=0)
    lower_bounds = jnp.cumsum(p_lb, axis=0) - p_lb[0:1]

    xp = x_prompt.reshape(n_p * len_p, D_MODEL)
    xs = x_sample.reshape(n_s * len_s, D_MODEL)
    outs_p, outs_s = [], []
    for l in range(depth):
        w = dict(norm_mix_w=norm_mix_w[l], w_in=w_in[l].astype(BF16), lb=lower_bounds[l],
                 a_norm_w=a_norm_w[l], q_norm_w=q_norm_w[l], k_norm_w=k_norm_w[l],
                 lambda_q1=lambda_q1[l], lambda_k1=lambda_k1[l],
                 lambda_q2=lambda_q2[l], lambda_k2=lambda_k2[l], b_subln_w=b_subln_w[l],
                 c_ln_w=c_ln_w[l], c_ln_b=c_ln_b[l], c_w_s=c_w_s[l], c_b_s=c_b_s[l],
                 w_branch_a=w_branch_a[l].astype(BF16), w_branch_b=w_branch_b[l].astype(BF16),
                 w_branch_c=w_branch_c[l].astype(BF16), w_out=w_out[l].astype(BF16),
                 norm_ffn_w=norm_ffn_w[l], w_up=w_up[l].astype(BF16), conv_w=conv_w[l],
                 conv_b=conv_b[l], w_down=w_down[l].astype(BF16))
        s0_p = jnp.zeros((n_p, N_HEADS, HEAD_W, HEAD_W), F32)
        conv0_p = jnp.zeros((n_p, CONV_W - 1, D_FF), F32)
        xp, *rest_p = _layer(xp, n_p, len_p, l, 0, s0_p, conv0_p, None, w, False)
        outs_p.append(rest_p)
        xs, *rest_s = _layer(xs, n_s, len_s, l, past_len, state_hgrn[l], state_conv[l],
                             (cache_k, cache_v, page_table), w, True)
        outs_s.append(rest_s)

    def stack(outs, idx, shape):
        return jnp.stack([o[idx].reshape(shape) for o in outs])

    kv_p = (n_p, len_p, N_HEADS, HEAD_W)
    kv_s = (n_s, len_s, N_HEADS, HEAD_W)
    return (xp.reshape(n_p, len_p, D_MODEL), xs.reshape(n_s, len_s, D_MODEL),
            stack(outs_p, 1, kv_p), stack(outs_p, 2, kv_p),
            stack(outs_p, 0, (n_p, N_HEADS, HEAD_W, HEAD_W)),
            stack(outs_p, 4, (n_p, CONV_W - 1, D_FF)),
            stack(outs_s, 1, kv_s), stack(outs_s, 2, kv_s),
            stack(outs_s, 0, (n_s, N_HEADS, HEAD_W, HEAD_W)),
            stack(outs_s, 4, (n_s, CONV_W - 1, D_FF)),
            stack(outs_s, 3, (n_s, len_s, BRANCH_W)))
```

```python
import functools
import math

import jax
import jax.numpy as jnp
from jax import lax
from jax.experimental import pallas as pl
from jax.experimental.pallas import tpu as pltpu

F32 = jnp.float32
BF16 = jnp.bfloat16

D_MODEL = 1024
BRANCH_W = 512
N_HEADS = 4
HEAD_W = 128
QK_DH = 64
ROPE_THETA = 10000.0
A_CHUNK = 64
C_CHUNK = 128
PAGE_SIZE = 128
D_FF = 2816
CONV_W = 3
IN_COLS = 9 * BRANCH_W + 3 * D_MODEL
EPS = 1e-6
NEG_INF = -1e30
LANES = 128
SUBLANES = 8

COL_AQ, COL_AF, COL_AI, COL_AG, COL_BQ, COL_BK, COL_BV, COL_CU, COL_CV = range(9)
GATE_BLOCK_W = 1536


def _sigmoid(x):
    return 1.0 / (1.0 + jnp.exp(-x))


def _gelu(x):
    c = math.sqrt(2.0 / math.pi)
    return 0.5 * x * (1.0 + jnp.tanh(c * (x + 0.044715 * (x * x * x))))


def _dot(a, b):
    return jnp.dot(a, b, preferred_element_type=F32)


def _dot_nt(a, b):
    return lax.dot_general(a, b, (((1,), (1,)), ((), ())), preferred_element_type=F32)


def _dot_tn(a, b):
    return lax.dot_general(a, b, (((0,), (0,)), ((), ())), preferred_element_type=F32)


def _cumsum_rows(x):
    row = lax.broadcasted_iota(jnp.int32, x.shape, 0)
    d = 1
    while d < x.shape[0]:
        x = x + jnp.where(row >= d, pltpu.roll(x, d, 0), 0.0)
        d *= 2
    return x


def _pad_rows(a, rows):
    if a.shape[0] >= rows:
        return a
    return jnp.concatenate([a, jnp.zeros((rows - a.shape[0],) + a.shape[1:], a.dtype)], axis=0)


def _log2(n):
    assert n & (n - 1) == 0, n
    return n.bit_length() - 1


def _inproj_kernel(x_ref, nw_ref, w_ref, z_ref, h_ref):
    @pl.when(pl.program_id(1) == 0)
    def _():
        x = x_ref[...]
        ms = jnp.mean(x * x, axis=-1, keepdims=True)
        h_ref[...] = (x * lax.rsqrt(ms + EPS) * nw_ref[...]).astype(BF16)

    z_ref[...] = _dot(h_ref[...], w_ref[...])


def _inproj(x, norm_w, w_bf16, tm, tn):
    t = x.shape[0]
    return pl.pallas_call(
        _inproj_kernel,
        grid=(t // tm, IN_COLS // tn),
        in_specs=[
            pl.BlockSpec((tm, D_MODEL), lambda i, j: (i, 0)),
            pl.BlockSpec((1, D_MODEL), lambda i, j: (0, 0)),
            pl.BlockSpec((D_MODEL, tn), lambda i, j: (0, j)),
        ],
        out_specs=pl.BlockSpec((tm, tn), lambda i, j: (i, j)),
        out_shape=jax.ShapeDtypeStruct((t, IN_COLS), F32),
        scratch_shapes=[pltpu.VMEM((tm, D_MODEL), BF16)],
        compiler_params=pltpu.CompilerParams(dimension_semantics=("parallel", "arbitrary")),
        name="inproj",
    )(x, norm_w.reshape(1, D_MODEL), w_bf16)


def _hgrn_kernel(zq_ref, zf_ref, zi_ref, zg_ref, lb_ref, nw_ref, s0_ref, oa_ref, sfin_ref,
                 st_ref, k_s, b_s, *, chunk, blk):
    c = pl.program_id(1)
    nc = pl.num_programs(1)

    @pl.when(c == 0)
    def _():
        for h in range(N_HEADS):
            st_ref[h] = s0_ref[0, h].T

    n_blk = chunk // blk
    t_loc = lax.broadcasted_iota(jnp.int32, (blk, HEAD_W), 0)
    lane = lax.broadcasted_iota(jnp.int32, (blk, LANES), 1)
    ones = jnp.ones((HEAD_W, LANES), BF16)
    mm_rows = max(chunk, 2 * SUBLANES)

    for h in range(N_HEADS):
        sl = slice(h * HEAD_W, (h + 1) * HEAD_W)
        lb = lb_ref[:, sl]
        sig = _sigmoid(zf_ref[:, sl])
        logf = jnp.log(lb + (1.0 - lb) * sig)
        k = (1.0 - lb) * (1.0 - sig)
        zq = zq_ref[:, sl]
        q = zq * _sigmoid(zq)
        v = zi_ref[:, sl]

        b = _cumsum_rows(logf)
        k_s[h] = k
        b_s[h] = b

        row_blocks = []
        for i in range(n_blk):
            r0 = i * blk
            q_i = q[r0:r0 + blk]
            b_i = b[r0:r0 + blk]
            terms = []
            for s in range(blk):
                ks = k_s[h, r0 + s:r0 + s + 1, :]
                bs = b_s[h, r0 + s:r0 + s + 1, :]
                dec = jnp.exp(jnp.where(t_loc >= s, b_i - bs, NEG_INF))
                terms.append(q_i * ks * dec)
            sums = _dot(jnp.concatenate(terms, axis=0).astype(BF16), ones)
            sc_i = jnp.zeros((blk, LANES), F32)
            for s in range(blk):
                sc_i = jnp.where(lane == r0 + s, sums[s * blk:(s + 1) * blk], sc_i)
            if i > 0:
                b0 = b_s[h, r0:r0 + 1, :]
                a_i = _pad_rows(q_i * jnp.exp(b_i - b0), 2 * SUBLANES).astype(BF16)
                k_left = _pad_rows(k[:r0] * jnp.exp(b0 - b[:r0]), LANES).astype(BF16)
                sc_i = sc_i + _dot_nt(a_i, k_left)[:blk]
            row_blocks.append(sc_i)
        scores = jnp.concatenate(row_blocks, axis=0)

        st = st_ref[h]
        v_pad = _pad_rows(v, LANES).astype(BF16)
        qd = _pad_rows(q * jnp.exp(b), mm_rows).astype(BF16)
        sc_bf = _pad_rows(scores, mm_rows).astype(BF16)
        o = (_dot_nt(qd, st.astype(BF16)) + _dot(sc_bf, v_pad))[:chunk]

        b_last = b[chunk - 1:chunk, :]
        kd = _pad_rows(k * jnp.exp(b_last - b), LANES).astype(BF16)
        st_ref[h] = jnp.exp(b_last) * st + _dot_tn(v_pad, kd)

        ms = jnp.mean(o * o, axis=-1, keepdims=True)
        on = o * lax.rsqrt(ms + EPS) * nw_ref[...]
        oa_ref[:, sl] = (on * _sigmoid(zg_ref[:, sl])).astype(oa_ref.dtype)

    @pl.when(c == nc - 1)
    def _():
        for h in range(N_HEADS):
            sfin_ref[0, h] = st_ref[h].T


def _hgrn(z, lb, norm_w, s0, nseq, seq_len, chunk, out_dtype):
    nc = seq_len // chunk
    zspec = lambda cb: pl.BlockSpec((chunk, BRANCH_W), lambda b, c: (b * nc + c, cb))
    return pl.pallas_call(
        functools.partial(_hgrn_kernel, chunk=chunk, blk=min(chunk, 2 * SUBLANES)),
        grid=(nseq, nc),
        in_specs=[
            zspec(COL_AQ), zspec(COL_AF), zspec(COL_AI), zspec(COL_AG),
            pl.BlockSpec((1, BRANCH_W), lambda b, c: (0, 0)),
            pl.BlockSpec((1, HEAD_W), lambda b, c: (0, 0)),
            pl.BlockSpec((1, N_HEADS, HEAD_W, HEAD_W), lambda b, c: (b, 0, 0, 0)),
        ],
        out_specs=[
            pl.BlockSpec((chunk, BRANCH_W), lambda b, c: (b * nc + c, 0)),
            pl.BlockSpec((1, N_HEADS, HEAD_W, HEAD_W), lambda b, c: (b, 0, 0, 0)),
        ],
        out_shape=[
            jax.ShapeDtypeStruct((nseq * seq_len, BRANCH_W), out_dtype),
            jax.ShapeDtypeStruct((nseq, N_HEADS, HEAD_W, HEAD_W), F32),
        ],
        scratch_shapes=[
            pltpu.VMEM((N_HEADS, HEAD_W, HEAD_W), F32),
            pltpu.VMEM((N_HEADS, chunk, HEAD_W), F32),
            pltpu.VMEM((N_HEADS, chunk, HEAD_W), F32),
        ],
        compiler_params=pltpu.CompilerParams(dimension_semantics=("parallel", "arbitrary")),
        name="hgrn",
    )(z, z, z, z, lb.reshape(1, BRANCH_W), norm_w.reshape(1, HEAD_W), s0)


def _qkv_kernel(zq_ref, zk_ref, zv_ref, cos_ref, sa_ref, sb_ref, qw_ref, kw_ref, g_ref,
                q_out, kf_out, vf_out, *bf_outs):
    cos = cos_ref[...]
    sa = sa_ref[...]
    sb = sb_ref[...]
    gmat = g_ref[...]

    def norm_rope(t, w):
        sq = t * t
        hi = sq.astype(BF16)
        lo = (sq - hi.astype(F32)).astype(BF16)
        ss = _dot(hi, gmat) + _dot(lo, gmat)
        y = t * lax.rsqrt(ss * (1.0 / QK_DH) + EPS) * w
        return (y * cos + pltpu.roll(y, LANES - QK_DH // 2, 1) * sa
                + pltpu.roll(y, QK_DH // 2, 1) * sb)

    for h in range(N_HEADS):
        sl = slice(h * HEAD_W, (h + 1) * HEAD_W)
        qr = norm_rope(zq_ref[:, sl], qw_ref[...])
        kr = norm_rope(zk_ref[:, sl], kw_ref[...])
        q_out[:, sl] = (qr * (QK_DH ** -0.5)).astype(q_out.dtype)
        rows = pl.ds(h, kr.shape[0], stride=N_HEADS)
        kf_out[rows, :] = kr
        v = zv_ref[:, sl]
        vf_out[rows, :] = v
        if bf_outs:
            bf_outs[0][:, sl] = kr.astype(BF16)
            bf_outs[1][:, sl] = v.astype(BF16)


def _qkv_prep(z, tabs, q_norm_w, k_norm_w, tm, tab_blocks, emit_bf16):
    t = z.shape[0]
    zspec = lambda cb: pl.BlockSpec((tm, BRANCH_W), lambda i: (i, cb))
    tspec = pl.BlockSpec((tm, LANES), lambda i: (i % tab_blocks, 0))
    wspec = pl.BlockSpec((1, LANES), lambda i: (0, 0))
    ospec = pl.BlockSpec((tm, BRANCH_W), lambda i: (i, 0))
    rspec = pl.BlockSpec((tm * N_HEADS, HEAD_W), lambda i: (i, 0))
    gi = lax.broadcasted_iota(jnp.int32, (LANES, LANES), 0) // QK_DH
    gj = lax.broadcasted_iota(jnp.int32, (LANES, LANES), 1) // QK_DH
    gmat = (gi == gj).astype(BF16)
    n_out = 5 if emit_bf16 else 3
    dts = [BF16 if emit_bf16 else F32, F32, F32, BF16, BF16][:n_out]
    return pl.pallas_call(
        _qkv_kernel,
        grid=(t // tm,),
        in_specs=[zspec(COL_BQ), zspec(COL_BK), zspec(COL_BV), tspec, tspec, tspec, wspec, wspec,
                  pl.BlockSpec((LANES, LANES), lambda i: (0, 0))],
        out_specs=[ospec, rspec, rspec] + [ospec] * (n_out - 3),
        out_shape=[jax.ShapeDtypeStruct((t * N_HEADS, HEAD_W) if i in (1, 2) else (t, BRANCH_W), d)
                   for i, d in enumerate(dts)],
        compiler_params=pltpu.CompilerParams(dimension_semantics=("parallel",)),
        name="qkv_prep",
    )(z, z, z, *tabs, jnp.tile(q_norm_w, 2).reshape(1, LANES),
      jnp.tile(k_norm_w, 2).reshape(1, LANES), gmat)


def _rope_tables(pos0, seq_len, rows):
    half = QK_DH // 2
    inv = ROPE_THETA ** (-jnp.arange(half, dtype=F32) / half)
    pos = (pos0 + jnp.arange(seq_len, dtype=jnp.int32)).astype(F32)
    ang = pos[:, None] * inv[None, :]
    cos, sin = jnp.cos(ang), jnp.sin(ang)
    zero = jnp.zeros_like(sin)
    per_head = lambda a, b: jnp.tile(jnp.concatenate([a, b], axis=-1), (rows // seq_len, LANES // QK_DH))
    return per_head(cos, cos), per_head(-sin, zero), per_head(zero, sin)


def _softmax_step(s, v_bf16, m_ref, l_ref, acc_ref):
    m_old = m_ref[...]
    m_new = jnp.maximum(m_old, jnp.max(s, axis=-1, keepdims=True))
    alpha = jnp.exp(m_old - m_new)
    p = jnp.exp(s - pltpu.repeat(m_new, s.shape[1] // LANES, axis=1))
    l_ref[...] = alpha * l_ref[...] + jnp.sum(p, axis=-1, keepdims=True)
    acc_ref[...] = alpha * acc_ref[...] + _dot(p.astype(BF16), v_bf16)
    m_ref[...] = m_new


def _subln(o, w, scale):
    ms = jnp.mean(o * o, axis=-1, keepdims=True)
    return o * lax.rsqrt(ms + EPS) * w * scale


def _attn_kernel(qi_ref, ki_ref, lam_ref, w_ref, q_ref, k_ref, v_ref, o_ref, m1, l1, a1, m2, l2, a2,
                 *, out_scale):
    qi = qi_ref[pl.program_id(2)]
    ki = ki_ref[pl.program_id(2)]

    @pl.when(ki == 0)
    def _():
        for m, l, a in ((m1, l1, a1), (m2, l2, a2)):
            m[...] = jnp.full(m.shape, NEG_INF, F32)
            l[...] = jnp.zeros(l.shape, F32)
            a[...] = jnp.zeros(a.shape, F32)

    def update(masked):
        q = q_ref[...]
        k = k_ref[...]
        v = v_ref[...]
        lane = lax.broadcasted_iota(jnp.int32, q.shape, 1)
        zero = jnp.zeros_like(q)
        s1 = _dot_nt(jnp.where(lane < QK_DH, q, zero), k)
        s2 = _dot_nt(jnp.where(lane >= QK_DH, q, zero), k)
        if masked:
            r = lax.broadcasted_iota(jnp.int32, s1.shape, 0)
            c = lax.broadcasted_iota(jnp.int32, s1.shape, 1)
            s1 = jnp.where(c <= r, s1, NEG_INF)
            s2 = jnp.where(c <= r, s2, NEG_INF)
        _softmax_step(s1, v, m1, l1, a1)
        _softmax_step(s2, v, m2, l2, a2)

    @pl.when(ki < qi)
    def _():
        update(False)

    @pl.when(ki == qi)
    def _():
        update(True)
        o = a1[...] / l1[...] - lam_ref[...] * (a2[...] / l2[...])
        o_ref[...] = _subln(o, w_ref[...], out_scale).astype(o_ref.dtype)


def _prompt_attn(q, k, v, lam, subln_w, nseq, seq_len, tq, out_scale, out_dtype):
    nq = seq_len // tq
    pairs = [(i, j) for i in range(nq) for j in range(i + 1)]
    qi_tab = jnp.array([p[0] for p in pairs], jnp.int32)
    ki_tab = jnp.array([p[1] for p in pairs], jnp.int32)
    qspec = pl.BlockSpec((tq, HEAD_W), lambda b, h, p, qt, kt: (b * nq + qt[p], h))
    kspec = pl.BlockSpec((tq, HEAD_W), lambda b, h, p, qt, kt: (b * nq + kt[p], h))
    cspec = pl.BlockSpec((1, HEAD_W), lambda b, h, p, qt, kt: (0, 0))
    stat = pltpu.VMEM((tq, LANES), F32)
    acc = pltpu.VMEM((tq, HEAD_W), F32)
    grid_spec = pltpu.PrefetchScalarGridSpec(
        num_scalar_prefetch=2,
        grid=(nseq, N_HEADS, len(pairs)),
        in_specs=[cspec, cspec, qspec, kspec, kspec],
        out_specs=qspec,
        scratch_shapes=[stat, stat, acc, stat, stat, acc],
    )
    return pl.pallas_call(
        functools.partial(_attn_kernel, out_scale=out_scale),
        grid_spec=grid_spec,
        out_shape=jax.ShapeDtypeStruct((nseq * seq_len, BRANCH_W), out_dtype),
        compiler_params=pltpu.CompilerParams(
            dimension_semantics=("parallel", "parallel", "arbitrary")),
        name="prompt_attn",
    )(qi_tab, ki_tab, jnp.full((1, HEAD_W), lam, F32), subln_w.reshape(1, HEAD_W), q, k, v)


def _sattn_kernel(pt_ref, lam_ref, w_ref, q_ref, kn_ref, vn_ref, *rest, pages, dec_seq, out_scale):
    del pt_ref
    k_refs = rest[:pages]
    v_refs = rest[pages:2 * pages]
    o_ref = rest[2 * pages]
    qall, m_ref, l_ref, acc_ref = rest[2 * pages + 1:]
    g = pl.program_id(1)
    ng = pl.num_programs(1)
    hr = 2 * dec_seq

    @pl.when(g == 0)
    def _():
        lane = lax.broadcasted_iota(jnp.int32, (dec_seq, HEAD_W), 1)
        parts = []
        for h in range(N_HEADS):
            qh = q_ref[:, h * HEAD_W:(h + 1) * HEAD_W]
            parts += [jnp.where(lane < QK_DH, qh, 0.0), jnp.where(lane >= QK_DH, qh, 0.0)]
        qall[...] = jnp.concatenate(parts, axis=0).astype(BF16)
        m_ref[...] = jnp.full(m_ref.shape, NEG_INF, F32)
        l_ref[...] = jnp.zeros(l_ref.shape, F32)
        acc_ref[...] = jnp.zeros(acc_ref.shape, F32)

    def step(k_of_head, v_of_head, mask):
        s = jnp.concatenate([_dot_nt(qall[h * hr:(h + 1) * hr, :], k_of_head(h))
                             for h in range(N_HEADS)], axis=0)
        if mask is not None:
            s = jnp.where(mask(s.shape), s, NEG_INF)
        m_old = m_ref[...]
        m_new = jnp.maximum(m_old, jnp.max(s, axis=-1, keepdims=True))
        alpha = jnp.exp(m_old - m_new)
        p = jnp.exp(s - m_new)
        l_ref[...] = alpha * l_ref[...] + jnp.sum(p, axis=-1, keepdims=True)
        pv = jnp.concatenate([_dot(p[h * hr:(h + 1) * hr].astype(BF16), v_of_head(h))
                              for h in range(N_HEADS)], axis=0)
        acc_ref[...] = alpha * acc_ref[...] + pv
        m_ref[...] = m_new

    def paged(refs):
        return lambda h: jnp.concatenate(
            [r[pl.ds(h, PAGE_SIZE, stride=N_HEADS), :] for r in refs], axis=0).astype(BF16)

    step(paged(k_refs), paged(v_refs), None)

    @pl.when(g == ng - 1)
    def _():
        def fresh(ref):
            return lambda h: _pad_rows(ref[pl.ds(h, dec_seq, stride=N_HEADS), :], LANES).astype(BF16)

        def causal(shape):
            tok = lax.broadcasted_iota(jnp.int32, shape, 0) & (dec_seq - 1)
            return lax.broadcasted_iota(jnp.int32, shape, 1) <= tok

        step(fresh(kn_ref), fresh(vn_ref), causal)
        acc = acc_ref[...]
        l = l_ref[...]
        for h in range(N_HEADS):
            r1 = slice(h * hr, h * hr + dec_seq)
            r2 = slice(h * hr + dec_seq, (h + 1) * hr)
            o = acc[r1] / l[r1] - lam_ref[...] * (acc[r2] / l[r2])
            o_ref[:, h * HEAD_W:(h + 1) * HEAD_W] = _subln(o, w_ref[...], out_scale).astype(o_ref.dtype)


def _sample_attn(q, k_new, v_new, cache_k, cache_v, page_table, layer, lam, subln_w, nseq, dec_seq,
                 pages, out_scale):
    n_pages = page_table.shape[1]
    rows = 2 * N_HEADS * dec_seq
    depth, n_pool = cache_k.shape[:2]
    cache_k = cache_k.reshape(depth, n_pool, PAGE_SIZE * N_HEADS, HEAD_W)
    cache_v = cache_v.reshape(depth, n_pool, PAGE_SIZE * N_HEADS, HEAD_W)

    def page_spec(i):
        return pl.BlockSpec((None, None, PAGE_SIZE * N_HEADS, HEAD_W),
                            lambda b, g, pt: (layer, pt[b, g * pages + i], 0, 0))

    cspec = pl.BlockSpec((1, HEAD_W), lambda b, g, pt: (0, 0))
    rspec = pl.BlockSpec((dec_seq, BRANCH_W), lambda b, g, pt: (b, 0))
    nspec = pl.BlockSpec((dec_seq * N_HEADS, HEAD_W), lambda b, g, pt: (b, 0))
    grid_spec = pltpu.PrefetchScalarGridSpec(
        num_scalar_prefetch=1,
        grid=(nseq, n_pages // pages),
        in_specs=[cspec, cspec, rspec, nspec, nspec]
        + [page_spec(i) for i in range(pages)] * 2,
        out_specs=rspec,
        scratch_shapes=[
            pltpu.VMEM((rows, HEAD_W), BF16),
            pltpu.VMEM((rows, 1), F32),
            pltpu.VMEM((rows, 1), F32),
            pltpu.VMEM((rows, HEAD_W), F32),
        ],
    )
    return pl.pallas_call(
        functools.partial(_sattn_kernel, pages=pages, dec_seq=dec_seq, out_scale=out_scale),
        grid_spec=grid_spec,
        out_shape=jax.ShapeDtypeStruct((nseq * dec_seq, BRANCH_W), F32),
        compiler_params=pltpu.CompilerParams(dimension_semantics=("parallel", "arbitrary")),
        name="sample_attn",
    )(page_table, jnp.full((1, HEAD_W), lam, F32), subln_w.reshape(1, HEAD_W), q, k_new, v_new,
      *([cache_k] * pages), *([cache_v] * pages))


def _tmlp_kernel(zu_ref, zv_ref, lnw_ref, lnb_ref, wm_ref, bias_ref, oc_ref, *vc_out,
                 mix_rows, period, n_sub):
    u = _gelu(zu_ref[...])
    g = _gelu(zv_ref[...])
    xc = g - jnp.mean(g, axis=-1, keepdims=True)
    vc = xc * lax.rsqrt(jnp.mean(xc * xc, axis=-1, keepdims=True) + EPS) * lnw_ref[...] + lnb_ref[...]
    if vc_out:
        vc_out[0][...] = vc
    row = lax.broadcasted_iota(jnp.int32, (mix_rows, mix_rows), 0)
    col = lax.broadcasted_iota(jnp.int32, (mix_rows, mix_rows), 1)
    sh = _log2(period)
    allow = ((row >> sh) == (col >> sh)) & ((col & (period - 1)) <= (row & (period - 1)))
    for gi in range(N_HEADS):
        sl = slice(gi * HEAD_W, (gi + 1) * HEAD_W)
        wm = jnp.where(allow, wm_ref[gi], 0.0).astype(BF16)
        for sb in range(n_sub):
            rs = slice(sb * mix_rows, (sb + 1) * mix_rows)
            mixed = _dot(wm, vc[rs, sl].astype(BF16)) + bias_ref[:, sl]
            oc_ref[rs, sl] = (u[rs, sl] * mixed).astype(oc_ref.dtype)


def _token_mlp(z, ln_w, ln_b, w_s, b_s, tile_rows, mix_rows, period, emit_v, out_dtype):
    t = z.shape[0]
    reps = mix_rows // period
    wm = jnp.tile(w_s[:, :period, :period], (1, reps, reps))
    bias = jnp.tile(jnp.repeat(b_s[:, :period].T, HEAD_W, axis=1), (reps, 1))
    zspec = lambda cb: pl.BlockSpec((tile_rows, BRANCH_W), lambda i: (i, cb))
    vec = pl.BlockSpec((1, BRANCH_W), lambda i: (0, 0))
    ospec = pl.BlockSpec((tile_rows, BRANCH_W), lambda i: (i, 0))
    out_shape = [jax.ShapeDtypeStruct((t, BRANCH_W), out_dtype)]
    if emit_v:
        out_shape.append(jax.ShapeDtypeStruct((t, BRANCH_W), F32))
    return pl.pallas_call(
        functools.partial(_tmlp_kernel, mix_rows=mix_rows, period=period, n_sub=tile_rows // mix_rows),
        grid=(t // tile_rows,),
        in_specs=[zspec(COL_CU), zspec(COL_CV), vec, vec,
                  pl.BlockSpec((N_HEADS, mix_rows, mix_rows), lambda i: (0, 0, 0)),
                  pl.BlockSpec((mix_rows, BRANCH_W), lambda i: (0, 0))],
        out_specs=[ospec] * len(out_shape),
        out_shape=out_shape,
        compiler_params=pltpu.CompilerParams(dimension_semantics=("parallel",)),
        name="token_mlp",
    )(z, z, ln_w.reshape(1, BRANCH_W), ln_b.reshape(1, BRANCH_W), wm, bias)


def _merge_kernel(oa_ref, ob_ref, oc_ref, zg0_ref, zg1_ref, x_ref, wa_ref, wb_ref, wc_ref, wo_ref,
                  xo_ref):
    pa = _dot(oa_ref[...].astype(BF16), wa_ref[...])
    pb = _dot(ob_ref[...].astype(BF16), wb_ref[...])
    pc = _dot(oc_ref[...].astype(BF16), wc_ref[...])
    zg0 = zg0_ref[...]
    zg1 = zg1_ref[...]
    split = 2 * D_MODEL - GATE_BLOCK_W
    g_a = _sigmoid(zg0[:, :D_MODEL])
    g_b = _sigmoid(jnp.concatenate([zg0[:, D_MODEL:], zg1[:, :split]], axis=1))
    g_c = _sigmoid(zg1[:, split:])
    merged = g_a * pa + g_b * pb + g_c * pc
    xo_ref[...] = x_ref[...] + _dot(merged.astype(BF16), wo_ref[...])


def _merge(oa, ob, oc, z, x, wa, wb, wc, wo, tm):
    t = x.shape[0]
    bspec = pl.BlockSpec((tm, BRANCH_W), lambda i: (i, 0))
    gspec = lambda cb: pl.BlockSpec((tm, GATE_BLOCK_W), lambda i: (i, cb))
    xspec = pl.BlockSpec((tm, D_MODEL), lambda i: (i, 0))
    wspec = pl.BlockSpec((BRANCH_W, D_MODEL), lambda i: (0, 0))
    return pl.pallas_call(
        _merge_kernel,
        grid=(t // tm,),
        in_specs=[bspec, bspec, bspec, gspec(3), gspec(4), xspec, wspec, wspec, wspec,
                  pl.BlockSpec((D_MODEL, D_MODEL), lambda i: (0, 0))],
        out_specs=xspec,
        out_shape=jax.ShapeDtypeStruct((t, D_MODEL), F32),
        compiler_params=pltpu.CompilerParams(dimension_semantics=("parallel",)),
        name="merge",
    )(oa, ob, oc, z, z, x, wa, wb, wc, wo)


def _ffn_kernel(x_ref, nw_ref, wa_ref, wb_ref, wd_ref, cw_ref, cb_ref, cbuf_ref, xo_ref, last_ref,
                h_s, acc_s, abuf, *, tm, shift, tiles_per_seq, pad):
    i = pl.program_id(0)
    j = pl.program_id(1)
    nff = pl.num_programs(1)
    halo = (CONV_W - 1) * shift

    @pl.when(j == 0)
    def _():
        x = x_ref[...]
        ms = jnp.mean(x * x, axis=-1, keepdims=True)
        h_s[...] = (x * lax.rsqrt(ms + EPS) * nw_ref[...]).astype(BF16)
        acc_s[...] = jnp.zeros(acc_s.shape, F32)

    @pl.when(i % tiles_per_seq == 0)
    def _():
        abuf[j, pad - halo:pad, :] = cbuf_ref[0]

    @pl.when(i % tiles_per_seq != 0)
    def _():
        abuf[j, pad - halo:pad, :] = abuf[j, pad + tm - halo:pad + tm, :]

    h = h_s[...]
    a = _dot(h, wa_ref[...])
    bb = _dot(h, wb_ref[...])
    abuf[j, pad:pad + tm, :] = a
    a1 = abuf[j, pad - shift:pad - shift + tm, :]
    a2 = abuf[j, pad - 2 * shift:pad - 2 * shift + tm, :]
    conv = cb_ref[...] + a2 * cw_ref[0:1, :] + a1 * cw_ref[1:2, :] + a * cw_ref[2:3, :]
    acc_s[...] += _dot((_gelu(conv) * bb).astype(BF16), wd_ref[...])
    last_ref[0] = a[tm - halo:, :]

    @pl.when(j == nff - 1)
    def _():
        xo_ref[...] = x_ref[...] + acc_s[...]


def _ffn(x, norm_w, w_up, w_down, conv_w, conv_b, conv_buf, tm, tff, shift, tiles_per_seq):
    t = x.shape[0]
    nff = D_FF // tff
    halo = (CONV_W - 1) * shift
    pad = -(-halo // SUBLANES) * SUBLANES
    n_tiles = t // tm
    return pl.pallas_call(
        functools.partial(_ffn_kernel, tm=tm, shift=shift, tiles_per_seq=tiles_per_seq, pad=pad),
        grid=(n_tiles, nff),
        in_specs=[
            pl.BlockSpec((tm, D_MODEL), lambda i, j: (i, 0)),
            pl.BlockSpec((1, D_MODEL), lambda i, j: (0, 0)),
            pl.BlockSpec((D_MODEL, tff), lambda i, j: (0, j)),
            pl.BlockSpec((D_MODEL, tff), lambda i, j: (0, nff + j)),
            pl.BlockSpec((tff, D_MODEL), lambda i, j: (j, 0)),
            pl.BlockSpec((CONV_W, tff), lambda i, j: (0, j)),
            pl.BlockSpec((1, tff), lambda i, j: (0, j)),
            pl.BlockSpec((1, halo, tff), lambda i, j: (i // tiles_per_seq, 0, j)),
        ],
        out_specs=[
            pl.BlockSpec((tm, D_MODEL), lambda i, j: (i, 0)),
            pl.BlockSpec((1, halo, tff), lambda i, j: (i, 0, j)),
        ],
        out_shape=[
            jax.ShapeDtypeStruct((t, D_MODEL), F32),
            jax.ShapeDtypeStruct((n_tiles, halo, D_FF), F32),
        ],
        scratch_shapes=[
            pltpu.VMEM((tm, D_MODEL), BF16),
            pltpu.VMEM((tm, D_MODEL), F32),
            pltpu.VMEM((nff, pad + tm, tff), F32),
        ],
        compiler_params=pltpu.CompilerParams(dimension_semantics=("arbitrary", "arbitrary")),
        name="conv_ffn",
    )(x, norm_w.reshape(1, D_MODEL), w_up, w_up, w_down, conv_w, conv_b.reshape(1, D_FF), conv_buf)


def _row_tile(t, want):
    return min(t, want)


def _layer(x, nseq, seq_len, layer, pos0, s0, conv_state, past, w, is_sample):
    t = nseq * seq_len
    tm = _row_tile(t, 512)
    act_dtype = F32 if is_sample else BF16
    z = _inproj(x, w["norm_mix_w"], w["w_in"], tm, 1920)

    chunk = min(A_CHUNK, seq_len)
    oa, s_new = _hgrn(z, w["lb"], w["a_norm_w"], s0, nseq, seq_len, chunk, act_dtype)

    tab_rows = max(seq_len, tm)
    tabs = _rope_tables(pos0, seq_len, tab_rows)
    qkv = _qkv_prep(z, tabs, w["q_norm_w"], w["k_norm_w"], tm, tab_rows // tm, not is_sample)
    lam_init = 0.8 - 0.6 * math.exp(-0.3 * layer)
    lam = (jnp.exp(jnp.sum(w["lambda_q1"] * w["lambda_k1"]))
           - jnp.exp(jnp.sum(w["lambda_q2"] * w["lambda_k2"])) + lam_init)
    if is_sample:
        q, k_rows, v_rows = qkv
        cache_k, cache_v, page_table = past
        ob = _sample_attn(q, k_rows, v_rows, cache_k, cache_v, page_table, layer, lam, w["b_subln_w"],
                          nseq, seq_len, 8, 1.0 - lam_init)
    else:
        q, k_rows, v_rows, k_bf, v_bf = qkv
        ob = _prompt_attn(q, k_bf, v_bf, lam, w["b_subln_w"], nseq, seq_len, min(seq_len, 512),
                          1.0 - lam_init, act_dtype)

    if is_sample:
        oc, vc = _token_mlp(z, w["c_ln_w"], w["c_ln_b"], w["c_w_s"], w["c_b_s"], t, t, seq_len, True,
                            act_dtype)
    else:
        (oc,) = _token_mlp(z, w["c_ln_w"], w["c_ln_b"], w["c_w_s"], w["c_b_s"], tm, C_CHUNK, C_CHUNK,
                           False, act_dtype)
        vc = None

    x_mid = _merge(oa, ob, oc, z, x, w["w_branch_a"], w["w_branch_b"], w["w_branch_c"], w["w_out"], tm)

    if is_sample:
        xt = x_mid.reshape(nseq, seq_len, D_MODEL).transpose(1, 0, 2).reshape(t, D_MODEL)
        cbuf = conv_state.transpose(1, 0, 2).reshape(1, (CONV_W - 1) * nseq, D_FF)
        xo, last = _ffn(xt, w["norm_ffn_w"], w["w_up"], w["w_down"], w["conv_w"], w["conv_b"], cbuf,
                        t, 1408, nseq, 1)
        x_new = xo.reshape(seq_len, nseq, D_MODEL).transpose(1, 0, 2).reshape(t, D_MODEL)
        new_conv = last.reshape(CONV_W - 1, nseq, D_FF).transpose(1, 0, 2)
    else:
        tiles_per_seq = seq_len // tm
        x_new, last = _ffn(x_mid, w["norm_ffn_w"], w["w_up"], w["w_down"], w["conv_w"], w["conv_b"],
                           conv_state, tm, 1408, 1, tiles_per_seq)
        new_conv = last.reshape(nseq, tiles_per_seq, CONV_W - 1, D_FF)[:, -1]
    return x_new, s_new, k_rows, v_rows, vc, new_conv


def kernel(x_prompt, x_sample, cache_k, cache_v, page_table, state_hgrn, state_conv,
           norm_mix_w, w_in, lb_logits, a_norm_w, q_norm_w, k_norm_w,
           lambda_q1, lambda_k1, lambda_q2, lambda_k2, b_subln_w,
           c_ln_w, c_ln_b, c_w_s, c_b_s, w_branch_a, w_branch_b, w_branch_c, w_out,
           norm_ffn_w, w_up, conv_w, conv_b, w_down):
    depth = w_in.shape[0]
    n_p, len_p, _ = x_prompt.shape
    n_s, len_s, _ = x_sample.shape
    past_len = page_table.shape[1] * PAGE_SIZE
    p_lb = jax.nn.softmax(lb_logits.astype(F32), axis=0)
    lower_bounds = jnp.cumsum(p_lb, axis=0) - p_lb[0:1]

    xp = x_prompt.reshape(n_p * len_p, D_MODEL)
    xs = x_sample.reshape(n_s * len_s, D_MODEL)
    outs_p, outs_s = [], []
    for l in range(depth):
        w = dict(norm_mix_w=norm_mix_w[l], w_in=w_in[l].astype(BF16), lb=lower_bounds[l],
                 a_norm_w=a_norm_w[l], q_norm_w=q_norm_w[l], k_norm_w=k_norm_w[l],
                 lambda_q1=lambda_q1[l], lambda_k1=lambda_k1[l],
                 lambda_q2=lambda_q2[l], lambda_k2=lambda_k2[l], b_subln_w=b_subln_w[l],
                 c_ln_w=c_ln_w[l], c_ln_b=c_ln_b[l], c_w_s=c_w_s[l], c_b_s=c_b_s[l],
                 w_branch_a=w_branch_a[l].astype(BF16), w_branch_b=w_branch_b[l].astype(BF16),
                 w_branch_c=w_branch_c[l].astype(BF16), w_out=w_out[l].astype(BF16),
                 norm_ffn_w=norm_ffn_w[l], w_up=w_up[l].astype(BF16), conv_w=conv_w[l],
                 conv_b=conv_b[l], w_down=w_down[l].astype(BF16))
        s0_p = jnp.zeros((n_p, N_HEADS, HEAD_W, HEAD_W), F32)
        conv0_p = jnp.zeros((n_p, CONV_W - 1, D_FF), F32)
        xp, *rest_p = _layer(xp, n_p, len_p, l, 0, s0_p, conv0_p, None, w, False)
        outs_p.append(rest_p)
        xs, *rest_s = _layer(xs, n_s, len_s, l, past_len, state_hgrn[l], state_conv[l],
                             (cache_k, cache_v, page_table), w, True)
        outs_s.append(rest_s)

    def stack(outs, idx, shape):
        return jnp.stack([o[idx].reshape(shape) for o in outs])

    kv_p = (n_p, len_p, N_HEADS, HEAD_W)
    kv_s = (n_s, len_s, N_HEADS, HEAD_W)
    return (xp.reshape(n_p, len_p, D_MODEL), xs.reshape(n_s, len_s, D_MODEL),
            stack(outs_p, 1, kv_p), stack(outs_p, 2, kv_p),
            stack(outs_p, 0, (n_p, N_HEADS, HEAD_W, HEAD_W)),
            stack(outs_p, 4, (n_p, CONV_W - 1, D_FF)),
            stack(outs_s, 1, kv_s), stack(outs_s, 2, kv_s),
            stack(outs_s, 0, (n_s, N_HEADS, HEAD_W, HEAD_W)),
            stack(outs_s, 4, (n_s, CONV_W - 1, D_FF)),
            stack(outs_s, 3, (n_s, len_s, BRANCH_W)))
```

```python
import functools
import math

import jax
import jax.numpy as jnp
from jax import lax
from jax.experimental import pallas as pl
from jax.experimental.pallas import tpu as pltpu

F32 = jnp.float32
BF16 = jnp.bfloat16

D_MODEL = 1024
BRANCH_W = 512
N_HEADS = 4
HEAD_W = 128
QK_DH = 64
ROPE_THETA = 10000.0
A_CHUNK = 64
C_CHUNK = 128
PAGE_SIZE = 128
D_FF = 2816
CONV_W = 3
IN_COLS = 9 * BRANCH_W + 3 * D_MODEL
EPS = 1e-6
NEG_INF = -1e30
LANES = 128
SUBLANES = 8

COL_AQ, COL_AF, COL_AI, COL_AG, COL_BQ, COL_BK, COL_BV, COL_CU, COL_CV = range(9)
GATE_BLOCK_W = 1536


def _sigmoid(x):
    return 1.0 / (1.0 + jnp.exp(-x))


def _gelu(x):
    c = math.sqrt(2.0 / math.pi)
    return 0.5 * x * (1.0 + jnp.tanh(c * (x + 0.044715 * (x * x * x))))


def _dot(a, b):
    return jnp.dot(a, b, preferred_element_type=F32)


def _dot_nt(a, b):
    return lax.dot_general(a, b, (((1,), (1,)), ((), ())), preferred_element_type=F32)


def _dot_tn(a, b):
    return lax.dot_general(a, b, (((0,), (0,)), ((), ())), preferred_element_type=F32)


def _cumsum_rows(x):
    row = lax.broadcasted_iota(jnp.int32, x.shape, 0)
    d = 1
    while d < x.shape[0]:
        x = x + jnp.where(row >= d, pltpu.roll(x, d, 0), 0.0)
        d *= 2
    return x


def _pad_rows(a, rows):
    if a.shape[0] >= rows:
        return a
    return jnp.concatenate([a, jnp.zeros((rows - a.shape[0],) + a.shape[1:], a.dtype)], axis=0)


def _log2(n):
    assert n & (n - 1) == 0, n
    return n.bit_length() - 1


def _inproj_kernel(x_ref, nw_ref, w_ref, z_ref, zf_ref, h_ref):
    j = pl.program_id(1)

    @pl.when(j == 0)
    def _():
        x = x_ref[...]
        ms = jnp.mean(x * x, axis=-1, keepdims=True)
        h_ref[...] = (x * lax.rsqrt(ms + EPS) * nw_ref[...]).astype(BF16)

    z = _dot(h_ref[...], w_ref[...])
    z_ref[...] = z.astype(z_ref.dtype)

    @pl.when(j == 0)
    def _():
        zf_ref[...] = z[:, COL_AF * BRANCH_W:(COL_AF + 1) * BRANCH_W]


def _inproj(x, norm_w, w_bf16, tm, tn, z_dtype):
    t = x.shape[0]
    assert tn >= (COL_AF + 1) * BRANCH_W
    return pl.pallas_call(
        _inproj_kernel,
        grid=(t // tm, IN_COLS // tn),
        in_specs=[
            pl.BlockSpec((tm, D_MODEL), lambda i, j: (i, 0)),
            pl.BlockSpec((1, D_MODEL), lambda i, j: (0, 0)),
            pl.BlockSpec((D_MODEL, tn), lambda i, j: (0, j)),
        ],
        out_specs=[pl.BlockSpec((tm, tn), lambda i, j: (i, j)),
                   pl.BlockSpec((tm, BRANCH_W), lambda i, j: (i, 0))],
        out_shape=[jax.ShapeDtypeStruct((t, IN_COLS), z_dtype),
                   jax.ShapeDtypeStruct((t, BRANCH_W), F32)],
        scratch_shapes=[pltpu.VMEM((tm, D_MODEL), BF16)],
        compiler_params=pltpu.CompilerParams(dimension_semantics=("parallel", "arbitrary")),
        name="inproj",
    )(x, norm_w.reshape(1, D_MODEL), w_bf16)


def _hgrn_kernel(zq_ref, zf_ref, zi_ref, zg_ref, lb_ref, nw_ref, s0_ref, oa_ref, sfin_ref,
                 st_ref, k_s, b_s, *, chunk, blk, n_sub):
    c = pl.program_id(1)
    nc = pl.num_programs(1)

    @pl.when(c == 0)
    def _():
        for h in range(N_HEADS):
            st_ref[h] = s0_ref[0, h].T

    n_blk = chunk // blk
    t_loc = lax.broadcasted_iota(jnp.int32, (blk, HEAD_W), 0)
    lane = lax.broadcasted_iota(jnp.int32, (blk, LANES), 1)
    ones = jnp.ones((HEAD_W, LANES), BF16)
    mm_rows = max(chunk, 2 * SUBLANES)

    def one_chunk(rows):
        for h in range(N_HEADS):
            sl = slice(h * HEAD_W, (h + 1) * HEAD_W)
            lb = lb_ref[:, sl]
            sig = _sigmoid(zf_ref[rows, sl])
            logf = jnp.log(lb + (1.0 - lb) * sig)
            k = (1.0 - lb) * (1.0 - sig)
            zq = zq_ref[rows, sl].astype(F32)
            q = zq * _sigmoid(zq)
            v = zi_ref[rows, sl].astype(F32)

            b = _cumsum_rows(logf)
            k_s[h] = k
            b_s[h] = b

            row_blocks = []
            for i in range(n_blk):
                r0 = i * blk
                q_i = q[r0:r0 + blk]
                b_i = b[r0:r0 + blk]
                terms = []
                for s in range(blk):
                    ks = k_s[h, r0 + s:r0 + s + 1, :]
                    bs = b_s[h, r0 + s:r0 + s + 1, :]
                    dec = jnp.exp(jnp.where(t_loc >= s, b_i - bs, NEG_INF))
                    terms.append(q_i * ks * dec)
                sums = _dot(jnp.concatenate(terms, axis=0).astype(BF16), ones)
                sc_i = jnp.zeros((blk, LANES), F32)
                for s in range(blk):
                    sc_i = jnp.where(lane == r0 + s, sums[s * blk:(s + 1) * blk], sc_i)
                if i > 0:
                    b0 = b_s[h, r0:r0 + 1, :]
                    a_i = _pad_rows(q_i * jnp.exp(b_i - b0), 2 * SUBLANES).astype(BF16)
                    k_left = _pad_rows(k[:r0] * jnp.exp(b0 - b[:r0]), LANES).astype(BF16)
                    sc_i = sc_i + _dot_nt(a_i, k_left)[:blk]
                row_blocks.append(sc_i)
            scores = jnp.concatenate(row_blocks, axis=0)

            st = st_ref[h]
            v_pad = _pad_rows(v, LANES).astype(BF16)
            qd = _pad_rows(q * jnp.exp(b), mm_rows).astype(BF16)
            sc_bf = _pad_rows(scores, mm_rows).astype(BF16)
            o = (_dot_nt(qd, st.astype(BF16)) + _dot(sc_bf, v_pad))[:chunk]

            b_last = b[chunk - 1:chunk, :]
            kd = _pad_rows(k * jnp.exp(b_last - b), LANES).astype(BF16)
            st_ref[h] = jnp.exp(b_last) * st + _dot_tn(v_pad, kd)

            ms = jnp.mean(o * o, axis=-1, keepdims=True)
            on = o * lax.rsqrt(ms + EPS) * nw_ref[...]
            oa_ref[rows, sl] = (on * _sigmoid(zg_ref[rows, sl].astype(F32))).astype(oa_ref.dtype)

    if n_sub == 1:
        one_chunk(pl.ds(0, chunk))
    else:
        def body(ci, carry):
            one_chunk(pl.ds(pl.multiple_of(ci * chunk, chunk), chunk))
            return carry

        lax.fori_loop(0, n_sub, body, 0)

    @pl.when(c == nc - 1)
    def _():
        for h in range(N_HEADS):
            sfin_ref[0, h] = st_ref[h].T


def _hgrn(z, zf, lb, norm_w, s0, nseq, seq_len, chunk, n_sub, out_dtype):
    rows = chunk * n_sub
    nc = seq_len // rows
    zspec = lambda cb: pl.BlockSpec((rows, BRANCH_W), lambda b, c: (b * nc + c, cb))
    return pl.pallas_call(
        functools.partial(_hgrn_kernel, chunk=chunk, blk=min(chunk, 2 * SUBLANES), n_sub=n_sub),
        grid=(nseq, nc),
        in_specs=[
            zspec(COL_AQ), zspec(0), zspec(COL_AI), zspec(COL_AG),
            pl.BlockSpec((1, BRANCH_W), lambda b, c: (0, 0)),
            pl.BlockSpec((1, HEAD_W), lambda b, c: (0, 0)),
            pl.BlockSpec((1, N_HEADS, HEAD_W, HEAD_W), lambda b, c: (b, 0, 0, 0)),
        ],
        out_specs=[
            pl.BlockSpec((rows, BRANCH_W), lambda b, c: (b * nc + c, 0)),
            pl.BlockSpec((1, N_HEADS, HEAD_W, HEAD_W), lambda b, c: (b, 0, 0, 0)),
        ],
        out_shape=[
            jax.ShapeDtypeStruct((nseq * seq_len, BRANCH_W), out_dtype),
            jax.ShapeDtypeStruct((nseq, N_HEADS, HEAD_W, HEAD_W), F32),
        ],
        scratch_shapes=[
            pltpu.VMEM((N_HEADS, HEAD_W, HEAD_W), F32),
            pltpu.VMEM((N_HEADS, chunk, HEAD_W), F32),
            pltpu.VMEM((N_HEADS, chunk, HEAD_W), F32),
        ],
        compiler_params=pltpu.CompilerParams(dimension_semantics=("parallel", "arbitrary")),
        name="hgrn",
    )(z, zf, z, z, lb.reshape(1, BRANCH_W), norm_w.reshape(1, HEAD_W), s0)


def _qkv_kernel(zq_ref, zk_ref, zv_ref, cos_ref, sa_ref, sb_ref, qw_ref, kw_ref, g_ref,
                q_out, kf_out, vf_out, *bf_outs):
    cos = cos_ref[...]
    sa = sa_ref[...]
    sb = sb_ref[...]
    gmat = g_ref[...]

    def norm_rope(t, w):
        sq = t * t
        hi = sq.astype(BF16)
        lo = (sq - hi.astype(F32)).astype(BF16)
        ss = _dot(hi, gmat) + _dot(lo, gmat)
        y = t * lax.rsqrt(ss * (1.0 / QK_DH) + EPS) * w
        return (y * cos + pltpu.roll(y, LANES - QK_DH // 2, 1) * sa
                + pltpu.roll(y, QK_DH // 2, 1) * sb)

    for h in range(N_HEADS):
        sl = slice(h * HEAD_W, (h + 1) * HEAD_W)
        qr = norm_rope(zq_ref[:, sl].astype(F32), qw_ref[...])
        kr = norm_rope(zk_ref[:, sl].astype(F32), kw_ref[...])
        q_out[:, sl] = (qr * (QK_DH ** -0.5)).astype(q_out.dtype)
        rows = pl.ds(h, kr.shape[0], stride=N_HEADS)
        kf_out[rows, :] = kr
        v = zv_ref[:, sl].astype(F32)
        vf_out[rows, :] = v
        if bf_outs:
            bf_outs[0][:, sl] = kr.astype(BF16)
            bf_outs[1][:, sl] = v.astype(BF16)


def _qkv_prep(z, tabs, q_norm_w, k_norm_w, tm, tab_blocks, emit_bf16):
    t = z.shape[0]
    zspec = lambda cb: pl.BlockSpec((tm, BRANCH_W), lambda i: (i, cb))
    tspec = pl.BlockSpec((tm, LANES), lambda i: (i % tab_blocks, 0))
    wspec = pl.BlockSpec((1, LANES), lambda i: (0, 0))
    ospec = pl.BlockSpec((tm, BRANCH_W), lambda i: (i, 0))
    rspec = pl.BlockSpec((tm * N_HEADS, HEAD_W), lambda i: (i, 0))
    gi = lax.broadcasted_iota(jnp.int32, (LANES, LANES), 0) // QK_DH
    gj = lax.broadcasted_iota(jnp.int32, (LANES, LANES), 1) // QK_DH
    gmat = (gi == gj).astype(BF16)
    n_out = 5 if emit_bf16 else 3
    dts = [BF16 if emit_bf16 else F32, F32, F32, BF16, BF16][:n_out]
    return pl.pallas_call(
        _qkv_kernel,
        grid=(t // tm,),
        in_specs=[zspec(COL_BQ), zspec(COL_BK), zspec(COL_BV), tspec, tspec, tspec, wspec, wspec,
                  pl.BlockSpec((LANES, LANES), lambda i: (0, 0))],
        out_specs=[ospec, rspec, rspec] + [ospec] * (n_out - 3),
        out_shape=[jax.ShapeDtypeStruct((t * N_HEADS, HEAD_W) if i in (1, 2) else (t, BRANCH_W), d)
                   for i, d in enumerate(dts)],
        compiler_params=pltpu.CompilerParams(dimension_semantics=("parallel",)),
        name="qkv_prep",
    )(z, z, z, *tabs, jnp.tile(q_norm_w, 2).reshape(1, LANES),
      jnp.tile(k_norm_w, 2).reshape(1, LANES), gmat)


def _rope_tables(pos0, seq_len, rows):
    half = QK_DH // 2
    inv = ROPE_THETA ** (-jnp.arange(half, dtype=F32) / half)
    pos = (pos0 + jnp.arange(seq_len, dtype=jnp.int32)).astype(F32)
    ang = pos[:, None] * inv[None, :]
    cos, sin = jnp.cos(ang), jnp.sin(ang)
    zero = jnp.zeros_like(sin)
    per_head = lambda a, b: jnp.tile(jnp.concatenate([a, b], axis=-1), (rows // seq_len, LANES // QK_DH))
    return per_head(cos, cos), per_head(-sin, zero), per_head(zero, sin)


def _softmax_step(s, v_bf16, m_ref, l_ref, acc_ref):
    m_old = m_ref[...]
    m_new = jnp.maximum(m_old, jnp.max(s, axis=-1, keepdims=True))
    alpha = jnp.exp(m_old - m_new)
    p = jnp.exp(s - jnp.concatenate([m_new] * (s.shape[1] // LANES), axis=1))
    l_ref[...] = alpha * l_ref[...] + jnp.sum(p, axis=-1, keepdims=True)
    acc_ref[...] = alpha * acc_ref[...] + _dot(p.astype(BF16), v_bf16)
    m_ref[...] = m_new


def _subln(o, w, scale):
    ms = jnp.mean(o * o, axis=-1, keepdims=True)
    return o * lax.rsqrt(ms + EPS) * w * scale


def _attn_kernel(lam_ref, w_ref, q_ref, k_ref, v_ref, o_ref, q1_s, q2_s, m1, l1, a1, m2, l2, a2,
                 *, tq, out_scale):
    nq = q_ref.shape[0] // tq

    def block_rows(i):
        return pl.ds(pl.multiple_of(i * tq, tq), tq)

    def update(ki, masked):
        k = k_ref[block_rows(ki), :]
        v = v_ref[block_rows(ki), :]
        s1 = _dot_nt(q1_s[...], k)
        s2 = _dot_nt(q2_s[...], k)
        if masked:
            r = lax.broadcasted_iota(jnp.int32, s1.shape, 0)
            c = lax.broadcasted_iota(jnp.int32, s1.shape, 1)
            s1 = jnp.where(c <= r, s1, NEG_INF)
            s2 = jnp.where(c <= r, s2, NEG_INF)
        _softmax_step(s1, v, m1, l1, a1)
        _softmax_step(s2, v, m2, l2, a2)

    def query_block(qi, carry):
        q = q_ref[block_rows(qi), :]
        lane = lax.broadcasted_iota(jnp.int32, q.shape, 1)
        zero = jnp.zeros_like(q)
        q1_s[...] = jnp.where(lane < QK_DH, q, zero)
        q2_s[...] = jnp.where(lane >= QK_DH, q, zero)
        for m, l, a in ((m1, l1, a1), (m2, l2, a2)):
            m[...] = jnp.full(m.shape, NEG_INF, F32)
            l[...] = jnp.zeros(l.shape, F32)
            a[...] = jnp.zeros(a.shape, F32)

        def key_block(ki, c):
            update(ki, False)
            return c

        lax.fori_loop(0, qi, key_block, 0)
        update(qi, True)
        o = a1[...] / l1[...] - lam_ref[...] * (a2[...] / l2[...])
        o_ref[block_rows(qi), :] = _subln(o, w_ref[...], out_scale).astype(o_ref.dtype)
        return carry

    lax.fori_loop(0, nq, query_block, 0)


def _prompt_attn(q, k, v, lam, subln_w, nseq, seq_len, tq, out_scale, out_dtype):
    sspec = pl.BlockSpec((seq_len, HEAD_W), lambda b, h: (b, h))
    cspec = pl.BlockSpec((1, HEAD_W), lambda b, h: (0, 0))
    qbuf = pltpu.VMEM((tq, HEAD_W), q.dtype)
    stat = pltpu.VMEM((tq, LANES), F32)
    acc = pltpu.VMEM((tq, HEAD_W), F32)
    return pl.pallas_call(
        functools.partial(_attn_kernel, tq=tq, out_scale=out_scale),
        grid=(nseq, N_HEADS),
        in_specs=[cspec, cspec, sspec, sspec, sspec],
        out_specs=sspec,
        out_shape=jax.ShapeDtypeStruct((nseq * seq_len, BRANCH_W), out_dtype),
        scratch_shapes=[qbuf, qbuf, stat, stat, acc, stat, stat, acc],
        compiler_params=pltpu.CompilerParams(dimension_semantics=("parallel", "parallel")),
        name="prompt_attn",
    )(jnp.full((1, HEAD_W), lam, F32), subln_w.reshape(1, HEAD_W), q, k, v)


def _sattn_kernel(pt_ref, lam_ref, w_ref, q_ref, kn_ref, vn_ref, *rest, pages, dec_seq, out_scale):
    del pt_ref
    k_refs = rest[:pages]
    v_refs = rest[pages:2 * pages]
    o_ref = rest[2 * pages]
    qall, m_ref, l_ref, acc_ref = rest[2 * pages + 1:]
    g = pl.program_id(1)
    ng = pl.num_programs(1)
    hr = 2 * dec_seq

    @pl.when(g == 0)
    def _():
        lane = lax.broadcasted_iota(jnp.int32, (dec_seq, HEAD_W), 1)
        parts = []
        for h in range(N_HEADS):
            qh = q_ref[:, h * HEAD_W:(h + 1) * HEAD_W]
            parts += [jnp.where(lane < QK_DH, qh, 0.0), jnp.where(lane >= QK_DH, qh, 0.0)]
        qall[...] = jnp.concatenate(parts, axis=0).astype(BF16)
        m_ref[...] = jnp.full(m_ref.shape, NEG_INF, F32)
        l_ref[...] = jnp.zeros(l_ref.shape, F32)
        acc_ref[...] = jnp.zeros(acc_ref.shape, F32)

    def step(k_of_head, v_of_head, mask):
        s = jnp.concatenate([_dot_nt(qall[h * hr:(h + 1) * hr, :], k_of_head(h))
                             for h in range(N_HEADS)], axis=0)
        if mask is not None:
            s = jnp.where(mask(s.shape), s, NEG_INF)
        m_old = m_ref[...]
        m_new = jnp.maximum(m_old, jnp.max(s, axis=-1, keepdims=True))
        alpha = jnp.exp(m_old - m_new)
        p = jnp.exp(s - m_new)
        l_ref[...] = alpha * l_ref[...] + jnp.sum(p, axis=-1, keepdims=True)
        pv = jnp.concatenate([_dot(p[h * hr:(h + 1) * hr].astype(BF16), v_of_head(h))
                              for h in range(N_HEADS)], axis=0)
        acc_ref[...] = alpha * acc_ref[...] + pv
        m_ref[...] = m_new

    def paged(refs):
        return lambda h: jnp.concatenate(
            [r[pl.ds(h, PAGE_SIZE, stride=N_HEADS), :] for r in refs], axis=0).astype(BF16)

    step(paged(k_refs), paged(v_refs), None)

    @pl.when(g == ng - 1)
    def _():
        def fresh(ref):
            return lambda h: _pad_rows(ref[pl.ds(h, dec_seq, stride=N_HEADS), :], LANES).astype(BF16)

        def causal(shape):
            tok = lax.broadcasted_iota(jnp.int32, shape, 0) & (dec_seq - 1)
            return lax.broadcasted_iota(jnp.int32, shape, 1) <= tok

        step(fresh(kn_ref), fresh(vn_ref), causal)
        acc = acc_ref[...]
        l = l_ref[...]
        for h in range(N_HEADS):
            r1 = slice(h * hr, h * hr + dec_seq)
            r2 = slice(h * hr + dec_seq, (h + 1) * hr)
            o = acc[r1] / l[r1] - lam_ref[...] * (acc[r2] / l[r2])
            o_ref[:, h * HEAD_W:(h + 1) * HEAD_W] = _subln(o, w_ref[...], out_scale).astype(o_ref.dtype)


def _sample_attn(q, k_new, v_new, cache_k, cache_v, page_table, layer, lam, subln_w, nseq, dec_seq,
                 pages, out_scale):
    n_pages = page_table.shape[1]
    rows = 2 * N_HEADS * dec_seq
    depth, n_pool = cache_k.shape[:2]
    cache_k = cache_k.reshape(depth, n_pool, PAGE_SIZE * N_HEADS, HEAD_W)
    cache_v = cache_v.reshape(depth, n_pool, PAGE_SIZE * N_HEADS, HEAD_W)

    def page_spec(i):
        return pl.BlockSpec((None, None, PAGE_SIZE * N_HEADS, HEAD_W),
                            lambda b, g, pt: (layer, pt[b, g * pages + i], 0, 0))

    cspec = pl.BlockSpec((1, HEAD_W), lambda b, g, pt: (0, 0))
    rspec = pl.BlockSpec((dec_seq, BRANCH_W), lambda b, g, pt: (b, 0))
    nspec = pl.BlockSpec((dec_seq * N_HEADS, HEAD_W), lambda b, g, pt: (b, 0))
    grid_spec = pltpu.PrefetchScalarGridSpec(
        num_scalar_prefetch=1,
        grid=(nseq, n_pages // pages),
        in_specs=[cspec, cspec, rspec, nspec, nspec]
        + [page_spec(i) for i in range(pages)] * 2,
        out_specs=rspec,
        scratch_shapes=[
            pltpu.VMEM((rows, HEAD_W), BF16),
            pltpu.VMEM((rows, 1), F32),
            pltpu.VMEM((rows, 1), F32),
            pltpu.VMEM((rows, HEAD_W), F32),
        ],
    )
    return pl.pallas_call(
        functools.partial(_sattn_kernel, pages=pages, dec_seq=dec_seq, out_scale=out_scale),
        grid_spec=grid_spec,
        out_shape=jax.ShapeDtypeStruct((nseq * dec_seq, BRANCH_W), F32),
        compiler_params=pltpu.CompilerParams(dimension_semantics=("parallel", "arbitrary")),
        name="sample_attn",
    )(page_table, jnp.full((1, HEAD_W), lam, F32), subln_w.reshape(1, HEAD_W), q, k_new, v_new,
      *([cache_k] * pages), *([cache_v] * pages))


def _tmlp_kernel(zu_ref, zv_ref, lnw_ref, lnb_ref, wm_ref, bias_ref, oc_ref, *vc_out,
                 mix_rows, period, n_sub):
    u = _gelu(zu_ref[...].astype(F32))
    g = _gelu(zv_ref[...].astype(F32))
    xc = g - jnp.mean(g, axis=-1, keepdims=True)
    vc = xc * lax.rsqrt(jnp.mean(xc * xc, axis=-1, keepdims=True) + EPS) * lnw_ref[...] + lnb_ref[...]
    if vc_out:
        vc_out[0][...] = vc
    row = lax.broadcasted_iota(jnp.int32, (mix_rows, mix_rows), 0)
    col = lax.broadcasted_iota(jnp.int32, (mix_rows, mix_rows), 1)
    sh = _log2(period)
    allow = ((row >> sh) == (col >> sh)) & ((col & (period - 1)) <= (row & (period - 1)))
    for gi in range(N_HEADS):
        sl = slice(gi * HEAD_W, (gi + 1) * HEAD_W)
        wm = jnp.where(allow, wm_ref[gi], 0.0).astype(BF16)
        for sb in range(n_sub):
            rs = slice(sb * mix_rows, (sb + 1) * mix_rows)
            mixed = _dot(wm, vc[rs, sl].astype(BF16)) + bias_ref[:, sl]
            oc_ref[rs, sl] = (u[rs, sl] * mixed).astype(oc_ref.dtype)


def _token_mlp(z, ln_w, ln_b, w_s, b_s, tile_rows, mix_rows, period, emit_v, out_dtype):
    t = z.shape[0]
    reps = mix_rows // period
    wm = jnp.tile(w_s[:, :period, :period], (1, reps, reps))
    bias = jnp.tile(jnp.repeat(b_s[:, :period].T, HEAD_W, axis=1), (reps, 1))
    zspec = lambda cb: pl.BlockSpec((tile_rows, BRANCH_W), lambda i: (i, cb))
    vec = pl.BlockSpec((1, BRANCH_W), lambda i: (0, 0))
    ospec = pl.BlockSpec((tile_rows, BRANCH_W), lambda i: (i, 0))
    out_shape = [jax.ShapeDtypeStruct((t, BRANCH_W), out_dtype)]
    if emit_v:
        out_shape.append(jax.ShapeDtypeStruct((t, BRANCH_W), F32))
    return pl.pallas_call(
        functools.partial(_tmlp_kernel, mix_rows=mix_rows, period=period, n_sub=tile_rows // mix_rows),
        grid=(t // tile_rows,),
        in_specs=[zspec(COL_CU), zspec(COL_CV), vec, vec,
                  pl.BlockSpec((N_HEADS, mix_rows, mix_rows), lambda i: (0, 0, 0)),
                  pl.BlockSpec((mix_rows, BRANCH_W), lambda i: (0, 0))],
        out_specs=[ospec] * len(out_shape),
        out_shape=out_shape,
        compiler_params=pltpu.CompilerParams(dimension_semantics=("parallel",)),
        name="token_mlp",
    )(z, z, ln_w.reshape(1, BRANCH_W), ln_b.reshape(1, BRANCH_W), wm, bias)


def _merge_kernel(oa_ref, ob_ref, oc_ref, zg0_ref, zg1_ref, x_ref, wa_ref, wb_ref, wc_ref, wo_ref,
                  xo_ref):
    pa = _dot(oa_ref[...].astype(BF16), wa_ref[...])
    pb = _dot(ob_ref[...].astype(BF16), wb_ref[...])
    pc = _dot(oc_ref[...].astype(BF16), wc_ref[...])
    zg0 = zg0_ref[...].astype(F32)
    zg1 = zg1_ref[...].astype(F32)
    split = 2 * D_MODEL - GATE_BLOCK_W
    g_a = _sigmoid(zg0[:, :D_MODEL])
    g_b = _sigmoid(jnp.concatenate([zg0[:, D_MODEL:], zg1[:, :split]], axis=1))
    g_c = _sigmoid(zg1[:, split:])
    merged = g_a * pa + g_b * pb + g_c * pc
    xo_ref[...] = x_ref[...] + _dot(merged.astype(BF16), wo_ref[...])


def _merge(oa, ob, oc, z, x, wa, wb, wc, wo, tm):
    t = x.shape[0]
    bspec = pl.BlockSpec((tm, BRANCH_W), lambda i: (i, 0))
    gspec = lambda cb: pl.BlockSpec((tm, GATE_BLOCK_W), lambda i: (i, cb))
    xspec = pl.BlockSpec((tm, D_MODEL), lambda i: (i, 0))
    wspec = pl.BlockSpec((BRANCH_W, D_MODEL), lambda i: (0, 0))
    return pl.pallas_call(
        _merge_kernel,
        grid=(t // tm,),
        in_specs=[bspec, bspec, bspec, gspec(3), gspec(4), xspec, wspec, wspec, wspec,
                  pl.BlockSpec((D_MODEL, D_MODEL), lambda i: (0, 0))],
        out_specs=xspec,
        out_shape=jax.ShapeDtypeStruct((t, D_MODEL), F32),
        compiler_params=pltpu.CompilerParams(dimension_semantics=("parallel",)),
        name="merge",
    )(oa, ob, oc, z, z, x, wa, wb, wc, wo)


def _ffn_kernel(x_ref, nw_ref, wa_ref, wb_ref, wd_ref, cw_ref, cb_ref, cbuf_ref, xo_ref, last_ref,
                h_s, acc_s, abuf, *, tm, shift, tiles_per_seq, pad):
    i = pl.program_id(0)
    j = pl.program_id(1)
    nff = pl.num_programs(1)
    halo = (CONV_W - 1) * shift

    @pl.when(j == 0)
    def _():
        x = x_ref[...]
        ms = jnp.mean(x * x, axis=-1, keepdims=True)
        h_s[...] = (x * lax.rsqrt(ms + EPS) * nw_ref[...]).astype(BF16)
        acc_s[...] = jnp.zeros(acc_s.shape, F32)

    @pl.when(i % tiles_per_seq == 0)
    def _():
        abuf[j, pad - halo:pad, :] = cbuf_ref[0]

    @pl.when(i % tiles_per_seq != 0)
    def _():
        abuf[j, pad - halo:pad, :] = abuf[j, pad + tm - halo:pad + tm, :]

    h = h_s[...]
    a = _dot(h, wa_ref[...])
    bb = _dot(h, wb_ref[...])
    abuf[j, pad:pad + tm, :] = a
    a1 = abuf[j, pad - shift:pad - shift + tm, :]
    a2 = abuf[j, pad - 2 * shift:pad - 2 * shift + tm, :]
    conv = cb_ref[...] + a2 * cw_ref[0:1, :] + a1 * cw_ref[1:2, :] + a * cw_ref[2:3, :]
    acc_s[...] += _dot((_gelu(conv) * bb).astype(BF16), wd_ref[...])
    last_ref[0] = a[tm - halo:, :]

    @pl.when(j == nff - 1)
    def _():
        xo_ref[...] = x_ref[...] + acc_s[...]


def _ffn(x, norm_w, w_up, w_down, conv_w, conv_b, conv_buf, tm, tff, shift, tiles_per_seq):
    t = x.shape[0]
    nff = D_FF // tff
    halo = (CONV_W - 1) * shift
    pad = -(-halo // SUBLANES) * SUBLANES
    n_tiles = t // tm
    return pl.pallas_call(
        functools.partial(_ffn_kernel, tm=tm, shift=shift, tiles_per_seq=tiles_per_seq, pad=pad),
        grid=(n_tiles, nff),
        in_specs=[
            pl.BlockSpec((tm, D_MODEL), lambda i, j: (i, 0)),
            pl.BlockSpec((1, D_MODEL), lambda i, j: (0, 0)),
            pl.BlockSpec((D_MODEL, tff), lambda i, j: (0, j)),
            pl.BlockSpec((D_MODEL, tff), lambda i, j: (0, nff + j)),
            pl.BlockSpec((tff, D_MODEL), lambda i, j: (j, 0)),
            pl.BlockSpec((CONV_W, tff), lambda i, j: (0, j)),
            pl.BlockSpec((1, tff), lambda i, j: (0, j)),
            pl.BlockSpec((1, halo, tff), lambda i, j: (i // tiles_per_seq, 0, j)),
        ],
        out_specs=[
            pl.BlockSpec((tm, D_MODEL), lambda i, j: (i, 0)),
            pl.BlockSpec((1, halo, tff), lambda i, j: (i, 0, j)),
        ],
        out_shape=[
            jax.ShapeDtypeStruct((t, D_MODEL), F32),
            jax.ShapeDtypeStruct((n_tiles, halo, D_FF), F32),
        ],
        scratch_shapes=[
            pltpu.VMEM((tm, D_MODEL), BF16),
            pltpu.VMEM((tm, D_MODEL), F32),
            pltpu.VMEM((nff, pad + tm, tff), F32),
        ],
        compiler_params=pltpu.CompilerParams(dimension_semantics=("arbitrary", "arbitrary")),
        name="conv_ffn",
    )(x, norm_w.reshape(1, D_MODEL), w_up, w_up, w_down, conv_w, conv_b.reshape(1, D_FF), conv_buf)


def _row_tile(t, want):
    return min(t, want)


def _layer(x, nseq, seq_len, layer, pos0, s0, conv_state, past, w, is_sample):
    t = nseq * seq_len
    tm = _row_tile(t, 512)
    act_dtype = F32 if is_sample else BF16
    z, zf = _inproj(x, w["norm_mix_w"], w["w_in"], _row_tile(t, 1024), 1920, act_dtype)

    chunk = min(A_CHUNK, seq_len)
    oa, s_new = _hgrn(z, zf, w["lb"], w["a_norm_w"], s0, nseq, seq_len, chunk,
                      math.gcd(seq_len // chunk, 8), act_dtype)

    tab_rows = max(seq_len, tm)
    tabs = _rope_tables(pos0, seq_len, tab_rows)
    qkv = _qkv_prep(z, tabs, w["q_norm_w"], w["k_norm_w"], tm, tab_rows // tm, not is_sample)
    lam_init = 0.8 - 0.6 * math.exp(-0.3 * layer)
    lam = (jnp.exp(jnp.sum(w["lambda_q1"] * w["lambda_k1"]))
           - jnp.exp(jnp.sum(w["lambda_q2"] * w["lambda_k2"])) + lam_init)
    if is_sample:
        q, k_rows, v_rows = qkv
        cache_k, cache_v, page_table = past
        ob = _sample_attn(q, k_rows, v_rows, cache_k, cache_v, page_table, layer, lam, w["b_subln_w"],
                          nseq, seq_len, math.gcd(page_table.shape[1], 16), 1.0 - lam_init)
    else:
        q, k_rows, v_rows, k_bf, v_bf = qkv
        ob = _prompt_attn(q, k_bf, v_bf, lam, w["b_subln_w"], nseq, seq_len, min(seq_len, 512),
                          1.0 - lam_init, act_dtype)

    if is_sample:
        oc, vc = _token_mlp(z, w["c_ln_w"], w["c_ln_b"], w["c_w_s"], w["c_b_s"], t, t, seq_len, True,
                            act_dtype)
    else:
        (oc,) = _token_mlp(z, w["c_ln_w"], w["c_ln_b"], w["c_w_s"], w["c_b_s"], tm, C_CHUNK, C_CHUNK,
                           False, act_dtype)
        vc = None

    x_mid = _merge(oa, ob, oc, z, x, w["w_branch_a"], w["w_branch_b"], w["w_branch_c"], w["w_out"], tm)

    if is_sample:
        xt = x_mid.reshape(nseq, seq_len, D_MODEL).transpose(1, 0, 2).reshape(t, D_MODEL)
        cbuf = conv_state.transpose(1, 0, 2).reshape(1, (CONV_W - 1) * nseq, D_FF)
        xo, last = _ffn(xt, w["norm_ffn_w"], w["w_up"], w["w_down"], w["conv_w"], w["conv_b"], cbuf,
                        t, 1408, nseq, 1)
        x_new = xo.reshape(seq_len, nseq, D_MODEL).transpose(1, 0, 2).reshape(t, D_MODEL)
        new_conv = last.reshape(CONV_W - 1, nseq, D_FF).transpose(1, 0, 2)
    else:
        tiles_per_seq = seq_len // tm
        x_new, last = _ffn(x_mid, w["norm_ffn_w"], w["w_up"], w["w_down"], w["conv_w"], w["conv_b"],
                           conv_state, tm, 1408, 1, tiles_per_seq)
        new_conv = last.reshape(nseq, tiles_per_seq, CONV_W - 1, D_FF)[:, -1]
    return x_new, s_new, k_rows, v_rows, vc, new_conv


def kernel(x_prompt, x_sample, cache_k, cache_v, page_table, state_hgrn, state_conv,
           norm_mix_w, w_in, lb_logits, a_norm_w, q_norm_w, k_norm_w,
           lambda_q1, lambda_k1, lambda_q2, lambda_k2, b_subln_w,
           c_ln_w, c_ln_b, c_w_s, c_b_s, w_branch_a, w_branch_b, w_branch_c, w_out,
           norm_ffn_w, w_up, conv_w, conv_b, w_down):
    depth = w_in.shape[0]
    n_p, len_p, _ = x_prompt.shape
    n_s, len_s, _ = x_sample.shape
    past_len = page_table.shape[1] * PAGE_SIZE
    p_lb = jax.nn.softmax(lb_logits.astype(F32), axis=0)
    lower_bounds = jnp.cumsum(p_lb, axis=0) - p_lb[0:1]

    xp = x_prompt.reshape(n_p * len_p, D_MODEL)
    xs = x_sample.reshape(n_s * len_s, D_MODEL)
    outs_p, outs_s = [], []
    for l in range(depth):
        w = dict(norm_mix_w=norm_mix_w[l], w_in=w_in[l].astype(BF16), lb=lower_bounds[l],
                 a_norm_w=a_norm_w[l], q_norm_w=q_norm_w[l], k_norm_w=k_norm_w[l],
                 lambda_q1=lambda_q1[l], lambda_k1=lambda_k1[l],
                 lambda_q2=lambda_q2[l], lambda_k2=lambda_k2[l], b_subln_w=b_subln_w[l],
                 c_ln_w=c_ln_w[l], c_ln_b=c_ln_b[l], c_w_s=c_w_s[l], c_b_s=c_b_s[l],
                 w_branch_a=w_branch_a[l].astype(BF16), w_branch_b=w_branch_b[l].astype(BF16),
                 w_branch_c=w_branch_c[l].astype(BF16), w_out=w_out[l].astype(BF16),
                 norm_ffn_w=norm_ffn_w[l], w_up=w_up[l].astype(BF16), conv_w=conv_w[l],
                 conv_b=conv_b[l], w_down=w_down[l].astype(BF16))
        s0_p = jnp.zeros((n_p, N_HEADS, HEAD_W, HEAD_W), F32)
        conv0_p = jnp.zeros((n_p, CONV_W - 1, D_FF), F32)
        xp, *rest_p = _layer(xp, n_p, len_p, l, 0, s0_p, conv0_p, None, w, False)
        outs_p.append(rest_p)
        xs, *rest_s = _layer(xs, n_s, len_s, l, past_len, state_hgrn[l], state_conv[l],
                             (cache_k, cache_v, page_table), w, True)
        outs_s.append(rest_s)

    def stack(outs, idx, shape):
        return jnp.stack([o[idx].reshape(shape) for o in outs])

    kv_p = (n_p, len_p, N_HEADS, HEAD_W)
    kv_s = (n_s, len_s, N_HEADS, HEAD_W)
    return (xp.reshape(n_p, len_p, D_MODEL), xs.reshape(n_s, len_s, D_MODEL),
            stack(outs_p, 1, kv_p), stack(outs_p, 2, kv_p),
            stack(outs_p, 0, (n_p, N_HEADS, HEAD_W, HEAD_W)),
            stack(outs_p, 4, (n_p, CONV_W - 1, D_FF)),
            stack(outs_s, 1, kv_s), stack(outs_s, 2, kv_s),
            stack(outs_s, 0, (n_s, N_HEADS, HEAD_W, HEAD_W)),
            stack(outs_s, 4, (n_s, CONV_W - 1, D_FF)),
            stack(outs_s, 3, (n_s, len_s, BRANCH_W)))
```

```python
import functools
import math

import jax
import jax.numpy as jnp
from jax import lax
from jax.experimental import pallas as pl
from jax.experimental.pallas import tpu as pltpu

F32 = jnp.float32
BF16 = jnp.bfloat16

D_MODEL = 1024
BRANCH_W = 512
N_HEADS = 4
HEAD_W = 128
QK_DH = 64
ROPE_THETA = 10000.0
A_CHUNK = 64
C_CHUNK = 128
PAGE_SIZE = 128
D_FF = 2816
CONV_W = 3
IN_COLS = 9 * BRANCH_W + 3 * D_MODEL
EPS = 1e-6
NEG_INF = -1e30
LANES = 128
SUBLANES = 8

COL_AQ, COL_AF, COL_AI, COL_AG, COL_BQ, COL_BK, COL_BV, COL_CU, COL_CV = range(9)
GATE_BLOCK_W = 1536


def _sigmoid(x):
    return 1.0 / (1.0 + jnp.exp(-x))


def _sigmoid_gate(x):
    return 0.5 * jnp.tanh(0.5 * x) + 0.5


def _gelu(x):
    c = math.sqrt(2.0 / math.pi)
    return 0.5 * x * (1.0 + jnp.tanh(c * (x + 0.044715 * (x * x * x))))


def _dot(a, b):
    return jnp.dot(a, b, preferred_element_type=F32)


def _dot_nt(a, b):
    return lax.dot_general(a, b, (((1,), (1,)), ((), ())), preferred_element_type=F32)


def _dot_tn(a, b):
    return lax.dot_general(a, b, (((0,), (0,)), ((), ())), preferred_element_type=F32)


def _cumsum_rows(x):
    row = lax.broadcasted_iota(jnp.int32, x.shape, 0)
    d = 1
    while d < x.shape[0]:
        x = x + jnp.where(row >= d, pltpu.roll(x, d, 0), 0.0)
        d *= 2
    return x


def _pad_rows(a, rows):
    if a.shape[0] >= rows:
        return a
    return jnp.concatenate([a, jnp.zeros((rows - a.shape[0],) + a.shape[1:], a.dtype)], axis=0)


def _log2(n):
    assert n & (n - 1) == 0, n
    return n.bit_length() - 1


def _inproj_kernel(x_ref, nw_ref, w_ref, z_ref, zf_ref, h_ref):
    j = pl.program_id(1)

    @pl.when(j == 0)
    def _():
        x = x_ref[...]
        ms = jnp.mean(x * x, axis=-1, keepdims=True)
        h_ref[...] = (x * lax.rsqrt(ms + EPS) * nw_ref[...]).astype(BF16)

    z = _dot(h_ref[...], w_ref[...])
    z_ref[...] = z.astype(z_ref.dtype)

    @pl.when(j == 0)
    def _():
        zf_ref[...] = z[:, COL_AF * BRANCH_W:(COL_AF + 1) * BRANCH_W]


def _inproj(x, norm_w, w_bf16, tm, tn, z_dtype):
    t = x.shape[0]
    assert tn >= (COL_AF + 1) * BRANCH_W
    return pl.pallas_call(
        _inproj_kernel,
        grid=(t // tm, IN_COLS // tn),
        in_specs=[
            pl.BlockSpec((tm, D_MODEL), lambda i, j: (i, 0)),
            pl.BlockSpec((1, D_MODEL), lambda i, j: (0, 0)),
            pl.BlockSpec((D_MODEL, tn), lambda i, j: (0, j)),
        ],
        out_specs=[pl.BlockSpec((tm, tn), lambda i, j: (i, j)),
                   pl.BlockSpec((tm, BRANCH_W), lambda i, j: (i, 0))],
        out_shape=[jax.ShapeDtypeStruct((t, IN_COLS), z_dtype),
                   jax.ShapeDtypeStruct((t, BRANCH_W), F32)],
        scratch_shapes=[pltpu.VMEM((tm, D_MODEL), BF16)],
        compiler_params=pltpu.CompilerParams(dimension_semantics=("parallel", "arbitrary")),
        name="inproj",
    )(x, norm_w.reshape(1, D_MODEL), w_bf16)


def _hgrn_kernel(zq_ref, zf_ref, zi_ref, zg_ref, lb_ref, nw_ref, s0_ref, oa_ref, sfin_ref,
                 st_ref, c_s, b_s, *, chunk, blk, n_sub):
    c = pl.program_id(1)
    nc = pl.num_programs(1)

    @pl.when(c == 0)
    def _():
        for h in range(N_HEADS):
            st_ref[h] = s0_ref[0, h].T

    n_blk = chunk // blk
    t_loc = lax.broadcasted_iota(jnp.int32, (blk, HEAD_W), 0)
    lane = lax.broadcasted_iota(jnp.int32, (blk, LANES), 1)
    ones = jnp.ones((HEAD_W, LANES), BF16)
    mm_rows = max(chunk, 2 * SUBLANES)

    def one_chunk(rows):
        lb = lb_ref[...]
        sig = _sigmoid(zf_ref[rows, :])
        k_all = (1.0 - lb) * (1.0 - sig)
        b_all = _cumsum_rows(jnp.log(lb + (1.0 - lb) * sig))
        c_all = jnp.log(k_all) - b_all
        for h in range(N_HEADS):
            sl = slice(h * HEAD_W, (h + 1) * HEAD_W)
            k = k_all[:, sl]
            b = b_all[:, sl]
            zq = zq_ref[rows, sl].astype(F32)
            q = zq * _sigmoid_gate(zq)
            v = zi_ref[rows, sl].astype(F32)
            c_s[h] = c_all[:, sl]
            b_s[h] = b

            row_blocks = []
            for i in range(n_blk):
                r0 = i * blk
                q_i = q[r0:r0 + blk]
                b_i = b[r0:r0 + blk]
                terms, starts, n_rows = [], [], 0
                for s in range(blk):
                    g0 = (s // SUBLANES) * SUBLANES
                    cs = c_s[h, r0 + s:r0 + s + 1, :]
                    e = jnp.exp(jnp.where(t_loc[g0:] >= s, b_i[g0:] + cs, NEG_INF))
                    terms.append(q_i[g0:] * e)
                    starts.append(n_rows - g0)
                    n_rows += blk - g0
                sums = _dot(jnp.concatenate(terms, axis=0).astype(BF16), ones)
                groups = []
                for g in range(0, blk, SUBLANES):
                    sc_g = jnp.zeros((SUBLANES, LANES), F32)
                    for s in range(min(blk, g + SUBLANES)):
                        rows_s = sums[starts[s] + g:starts[s] + g + SUBLANES]
                        sc_g = jnp.where(lane[:SUBLANES] == r0 + s, rows_s, sc_g)
                    groups.append(sc_g)
                sc_i = jnp.concatenate(groups, axis=0)
                if i > 0:
                    b0 = b_s[h, r0:r0 + 1, :]
                    a_i = _pad_rows(q_i * jnp.exp(b_i - b0), 2 * SUBLANES).astype(BF16)
                    k_left = _pad_rows(k[:r0] * jnp.exp(b0 - b[:r0]), LANES).astype(BF16)
                    sc_i = sc_i + _dot_nt(a_i, k_left)[:blk]
                row_blocks.append(sc_i)
            scores = jnp.concatenate(row_blocks, axis=0)

            st = st_ref[h]
            v_pad = _pad_rows(v, LANES).astype(BF16)
            qd = _pad_rows(q * jnp.exp(b), mm_rows).astype(BF16)
            sc_bf = _pad_rows(scores, mm_rows).astype(BF16)
            o = (_dot_nt(qd, st.astype(BF16)) + _dot(sc_bf, v_pad))[:chunk]

            b_last = b[chunk - 1:chunk, :]
            kd = _pad_rows(k * jnp.exp(b_last - b), LANES).astype(BF16)
            st_ref[h] = jnp.exp(b_last) * st + _dot_tn(v_pad, kd)

            ms = jnp.mean(o * o, axis=-1, keepdims=True)
            on = o * lax.rsqrt(ms + EPS) * nw_ref[...]
            oa_ref[rows, sl] = (on * _sigmoid_gate(zg_ref[rows, sl].astype(F32))).astype(oa_ref.dtype)

    if n_sub == 1:
        one_chunk(pl.ds(0, chunk))
    else:
        def body(ci, carry):
            one_chunk(pl.ds(pl.multiple_of(ci * chunk, chunk), chunk))
            return carry

        lax.fori_loop(0, n_sub, body, 0, unroll=4)

    @pl.when(c == nc - 1)
    def _():
        for h in range(N_HEADS):
            sfin_ref[0, h] = st_ref[h].T


def _hgrn(z, zf, lb, norm_w, s0, nseq, seq_len, chunk, n_sub, out_dtype):
    rows = chunk * n_sub
    nc = seq_len // rows
    zspec = lambda cb: pl.BlockSpec((rows, BRANCH_W), lambda b, c: (b * nc + c, cb))
    return pl.pallas_call(
        functools.partial(_hgrn_kernel, chunk=chunk, blk=min(chunk, 2 * SUBLANES), n_sub=n_sub),
        grid=(nseq, nc),
        in_specs=[
            zspec(COL_AQ), zspec(0), zspec(COL_AI), zspec(COL_AG),
            pl.BlockSpec((1, BRANCH_W), lambda b, c: (0, 0)),
            pl.BlockSpec((1, HEAD_W), lambda b, c: (0, 0)),
            pl.BlockSpec((1, N_HEADS, HEAD_W, HEAD_W), lambda b, c: (b, 0, 0, 0)),
        ],
        out_specs=[
            pl.BlockSpec((rows, BRANCH_W), lambda b, c: (b * nc + c, 0)),
            pl.BlockSpec((1, N_HEADS, HEAD_W, HEAD_W), lambda b, c: (b, 0, 0, 0)),
        ],
        out_shape=[
            jax.ShapeDtypeStruct((nseq * seq_len, BRANCH_W), out_dtype),
            jax.ShapeDtypeStruct((nseq, N_HEADS, HEAD_W, HEAD_W), F32),
        ],
        scratch_shapes=[
            pltpu.VMEM((N_HEADS, HEAD_W, HEAD_W), F32),
            pltpu.VMEM((N_HEADS, chunk, HEAD_W), F32),
            pltpu.VMEM((N_HEADS, chunk, HEAD_W), F32),
        ],
        compiler_params=pltpu.CompilerParams(dimension_semantics=("parallel", "arbitrary")),
        name="hgrn",
    )(z, zf, z, z, lb.reshape(1, BRANCH_W), norm_w.reshape(1, HEAD_W), s0)


def _qkv_kernel(zq_ref, zk_ref, zv_ref, cos_ref, sa_ref, sb_ref, qw_ref, kw_ref, g_ref,
                q_out, kf_out, vf_out, *bf_outs):
    cos = cos_ref[...]
    sa = sa_ref[...]
    sb = sb_ref[...]
    gmat = g_ref[...]

    def norm_rope(t, w):
        sq = t * t
        hi = sq.astype(BF16)
        lo = (sq - hi.astype(F32)).astype(BF16)
        ss = _dot(hi, gmat) + _dot(lo, gmat)
        y = t * lax.rsqrt(ss * (1.0 / QK_DH) + EPS) * w
        return (y * cos + pltpu.roll(y, LANES - QK_DH // 2, 1) * sa
                + pltpu.roll(y, QK_DH // 2, 1) * sb)

    for h in range(N_HEADS):
        sl = slice(h * HEAD_W, (h + 1) * HEAD_W)
        qr = norm_rope(zq_ref[:, sl].astype(F32), qw_ref[...])
        kr = norm_rope(zk_ref[:, sl].astype(F32), kw_ref[...])
        q_out[:, sl] = (qr * (QK_DH ** -0.5)).astype(q_out.dtype)
        rows = pl.ds(h, kr.shape[0], stride=N_HEADS)
        kf_out[rows, :] = kr
        v = zv_ref[:, sl].astype(F32)
        vf_out[rows, :] = v
        if bf_outs:
            bf_outs[0][:, sl] = kr.astype(BF16)
            bf_outs[1][:, sl] = v.astype(BF16)


def _qkv_prep(z, tabs, q_norm_w, k_norm_w, tm, tab_blocks, emit_bf16):
    t = z.shape[0]
    zspec = lambda cb: pl.BlockSpec((tm, BRANCH_W), lambda i: (i, cb))
    tspec = pl.BlockSpec((tm, LANES), lambda i: (i % tab_blocks, 0))
    wspec = pl.BlockSpec((1, LANES), lambda i: (0, 0))
    ospec = pl.BlockSpec((tm, BRANCH_W), lambda i: (i, 0))
    rspec = pl.BlockSpec((tm * N_HEADS, HEAD_W), lambda i: (i, 0))
    gi = lax.broadcasted_iota(jnp.int32, (LANES, LANES), 0) // QK_DH
    gj = lax.broadcasted_iota(jnp.int32, (LANES, LANES), 1) // QK_DH
    gmat = (gi == gj).astype(BF16)
    n_out = 5 if emit_bf16 else 3
    dts = [BF16 if emit_bf16 else F32, F32, F32, BF16, BF16][:n_out]
    return pl.pallas_call(
        _qkv_kernel,
        grid=(t // tm,),
        in_specs=[zspec(COL_BQ), zspec(COL_BK), zspec(COL_BV), tspec, tspec, tspec, wspec, wspec,
                  pl.BlockSpec((LANES, LANES), lambda i: (0, 0))],
        out_specs=[ospec, rspec, rspec] + [ospec] * (n_out - 3),
        out_shape=[jax.ShapeDtypeStruct((t * N_HEADS, HEAD_W) if i in (1, 2) else (t, BRANCH_W), d)
                   for i, d in enumerate(dts)],
        compiler_params=pltpu.CompilerParams(dimension_semantics=("parallel",)),
        name="qkv_prep",
    )(z, z, z, *tabs, jnp.tile(q_norm_w, 2).reshape(1, LANES),
      jnp.tile(k_norm_w, 2).reshape(1, LANES), gmat)


def _rope_tables(pos0, seq_len, rows):
    half = QK_DH // 2
    inv = ROPE_THETA ** (-jnp.arange(half, dtype=F32) / half)
    pos = (pos0 + jnp.arange(seq_len, dtype=jnp.int32)).astype(F32)
    ang = pos[:, None] * inv[None, :]
    cos, sin = jnp.cos(ang), jnp.sin(ang)
    zero = jnp.zeros_like(sin)
    per_head = lambda a, b: jnp.tile(jnp.concatenate([a, b], axis=-1), (rows // seq_len, LANES // QK_DH))
    return per_head(cos, cos), per_head(-sin, zero), per_head(zero, sin)


def _softmax_step(s, v_bf16, m_ref, l_ref, acc_ref):
    m_old = m_ref[...]
    m_new = jnp.maximum(m_old, jnp.max(s, axis=-1, keepdims=True))
    alpha = jnp.exp(m_old - m_new)
    p = jnp.exp((s - jnp.concatenate([m_new] * (s.shape[1] // LANES), axis=1)).astype(BF16))
    l_ref[...] = alpha * l_ref[...] + jnp.sum(p.astype(F32), axis=-1, keepdims=True)
    acc_ref[...] = alpha * acc_ref[...] + _dot(p, v_bf16)
    m_ref[...] = m_new


def _subln(o, w, scale):
    ms = jnp.mean(o * o, axis=-1, keepdims=True)
    return o * lax.rsqrt(ms + EPS) * w * scale


def _attn_kernel(lam_ref, w_ref, q_ref, k_ref, v_ref, o_ref, q1_s, q2_s, m1, l1, a1, m2, l2, a2,
                 *, tq, tk, out_scale):
    nq = q_ref.shape[0] // tq
    kpq = tq // tk

    def q_rows(i):
        return pl.ds(pl.multiple_of(i * tq, tq), tq)

    def update(qi, ki, masked):
        k_rows = pl.ds(pl.multiple_of(ki * tk, tk), tk)
        k = k_ref[k_rows, :]
        v = v_ref[k_rows, :]
        s1 = _dot_nt(q1_s[...], k)
        s2 = _dot_nt(q2_s[...], k)
        if masked:
            r = lax.broadcasted_iota(jnp.int32, s1.shape, 0) + qi * tq
            c = lax.broadcasted_iota(jnp.int32, s1.shape, 1) + ki * tk
            s1 = jnp.where(c <= r, s1, NEG_INF)
            s2 = jnp.where(c <= r, s2, NEG_INF)
        _softmax_step(s1, v, m1, l1, a1)
        _softmax_step(s2, v, m2, l2, a2)

    def query_block(qi, carry):
        q = q_ref[q_rows(qi), :]
        lane = lax.broadcasted_iota(jnp.int32, q.shape, 1)
        zero = jnp.zeros_like(q)
        q1_s[...] = jnp.where(lane < QK_DH, q, zero)
        q2_s[...] = jnp.where(lane >= QK_DH, q, zero)
        for m, l, a in ((m1, l1, a1), (m2, l2, a2)):
            m[...] = jnp.full(m.shape, NEG_INF, F32)
            l[...] = jnp.zeros(l.shape, F32)
            a[...] = jnp.zeros(a.shape, F32)

        def key_block_pair(j, c):
            update(qi, 2 * j, False)
            update(qi, 2 * j + 1, False)
            return c

        n_past = qi * kpq
        lax.fori_loop(0, n_past // 2, key_block_pair, 0)

        @pl.when(n_past % 2 == 1)
        def _():
            update(qi, n_past - 1, False)

        for d in range(kpq):
            update(qi, qi * kpq + d, True)
        o = a1[...] / l1[...] - lam_ref[...] * (a2[...] / l2[...])
        o_ref[q_rows(qi), :] = _subln(o, w_ref[...], out_scale).astype(o_ref.dtype)
        return carry

    lax.fori_loop(0, nq, query_block, 0)


def _prompt_attn(q, k, v, lam, subln_w, nseq, seq_len, tq, tk, out_scale, out_dtype):
    sspec = pl.BlockSpec((seq_len, HEAD_W), lambda b, h: (b, h))
    cspec = pl.BlockSpec((1, HEAD_W), lambda b, h: (0, 0))
    qbuf = pltpu.VMEM((tq, HEAD_W), q.dtype)
    stat = pltpu.VMEM((tq, LANES), F32)
    acc = pltpu.VMEM((tq, HEAD_W), F32)
    return pl.pallas_call(
        functools.partial(_attn_kernel, tq=tq, tk=tk, out_scale=out_scale),
        grid=(nseq, N_HEADS),
        in_specs=[cspec, cspec, sspec, sspec, sspec],
        out_specs=sspec,
        out_shape=jax.ShapeDtypeStruct((nseq * seq_len, BRANCH_W), out_dtype),
        scratch_shapes=[qbuf, qbuf, stat, stat, acc, stat, stat, acc],
        compiler_params=pltpu.CompilerParams(dimension_semantics=("parallel", "parallel")),
        name="prompt_attn",
    )(jnp.full((1, HEAD_W), lam, F32), subln_w.reshape(1, HEAD_W), q, k, v)


def _sattn_kernel(pt_ref, lam_ref, w_ref, q_ref, kn_ref, vn_ref, *rest, pages, dec_seq, out_scale):
    del pt_ref
    k_refs = rest[:pages]
    v_refs = rest[pages:2 * pages]
    o_ref = rest[2 * pages]
    qall, m_ref, l_ref, acc_ref = rest[2 * pages + 1:]
    g = pl.program_id(1)
    ng = pl.num_programs(1)
    hr = 2 * dec_seq

    @pl.when(g == 0)
    def _():
        lane = lax.broadcasted_iota(jnp.int32, (dec_seq, HEAD_W), 1)
        parts = []
        for h in range(N_HEADS):
            qh = q_ref[:, h * HEAD_W:(h + 1) * HEAD_W]
            parts += [jnp.where(lane < QK_DH, qh, 0.0), jnp.where(lane >= QK_DH, qh, 0.0)]
        qall[...] = jnp.concatenate(parts, axis=0).astype(BF16)
        m_ref[...] = jnp.full(m_ref.shape, NEG_INF, F32)
        l_ref[...] = jnp.zeros(l_ref.shape, F32)
        acc_ref[...] = jnp.zeros(acc_ref.shape, F32)

    def step(k_of_head, v_of_head, mask):
        s = jnp.concatenate([_dot_nt(qall[h * hr:(h + 1) * hr, :], k_of_head(h))
                             for h in range(N_HEADS)], axis=0)
        if mask is not None:
            s = jnp.where(mask(s.shape), s, NEG_INF)
        m_old = m_ref[...]
        m_new = jnp.maximum(m_old, jnp.max(s, axis=-1, keepdims=True))
        alpha = jnp.exp(m_old - m_new)
        p = jnp.exp(s - m_new)
        l_ref[...] = alpha * l_ref[...] + jnp.sum(p, axis=-1, keepdims=True)
        pv = jnp.concatenate([_dot(p[h * hr:(h + 1) * hr].astype(BF16), v_of_head(h))
                              for h in range(N_HEADS)], axis=0)
        acc_ref[...] = alpha * acc_ref[...] + pv
        m_ref[...] = m_new

    def paged(refs):
        return lambda h: jnp.concatenate(
            [r[pl.ds(h, PAGE_SIZE, stride=N_HEADS), :] for r in refs], axis=0).astype(BF16)

    step(paged(k_refs), paged(v_refs), None)

    @pl.when(g == ng - 1)
    def _():
        def fresh(ref):
            return lambda h: _pad_rows(ref[pl.ds(h, dec_seq, stride=N_HEADS), :], LANES).astype(BF16)

        def causal(shape):
            tok = lax.broadcasted_iota(jnp.int32, shape, 0) & (dec_seq - 1)
            return lax.broadcasted_iota(jnp.int32, shape, 1) <= tok

        step(fresh(kn_ref), fresh(vn_ref), causal)
        acc = acc_ref[...]
        l = l_ref[...]
        for h in range(N_HEADS):
            r1 = slice(h * hr, h * hr + dec_seq)
            r2 = slice(h * hr + dec_seq, (h + 1) * hr)
            o = acc[r1] / l[r1] - lam_ref[...] * (acc[r2] / l[r2])
            o_ref[:, h * HEAD_W:(h + 1) * HEAD_W] = _subln(o, w_ref[...], out_scale).astype(o_ref.dtype)


def _sample_attn(q, k_new, v_new, cache_k, cache_v, page_table, layer, lam, subln_w, nseq, dec_seq,
                 pages, out_scale):
    n_pages = page_table.shape[1]
    rows = 2 * N_HEADS * dec_seq
    depth, n_pool = cache_k.shape[:2]
    cache_k = cache_k.reshape(depth, n_pool, PAGE_SIZE * N_HEADS, HEAD_W)
    cache_v = cache_v.reshape(depth, n_pool, PAGE_SIZE * N_HEADS, HEAD_W)

    def page_spec(i):
        return pl.BlockSpec((None, None, PAGE_SIZE * N_HEADS, HEAD_W),
                            lambda b, g, pt: (layer, pt[b, g * pages + i], 0, 0))

    cspec = pl.BlockSpec((1, HEAD_W), lambda b, g, pt: (0, 0))
    rspec = pl.BlockSpec((dec_seq, BRANCH_W), lambda b, g, pt: (b, 0))
    nspec = pl.BlockSpec((dec_seq * N_HEADS, HEAD_W), lambda b, g, pt: (b, 0))
    grid_spec = pltpu.PrefetchScalarGridSpec(
        num_scalar_prefetch=1,
        grid=(nseq, n_pages // pages),
        in_specs=[cspec, cspec, rspec, nspec, nspec]
        + [page_spec(i) for i in range(pages)] * 2,
        out_specs=rspec,
        scratch_shapes=[
            pltpu.VMEM((rows, HEAD_W), BF16),
            pltpu.VMEM((rows, 1), F32),
            pltpu.VMEM((rows, 1), F32),
            pltpu.VMEM((rows, HEAD_W), F32),
        ],
    )
    return pl.pallas_call(
        functools.partial(_sattn_kernel, pages=pages, dec_seq=dec_seq, out_scale=out_scale),
        grid_spec=grid_spec,
        out_shape=jax.ShapeDtypeStruct((nseq * dec_seq, BRANCH_W), F32),
        compiler_params=pltpu.CompilerParams(dimension_semantics=("parallel", "arbitrary")),
        name="sample_attn",
    )(page_table, jnp.full((1, HEAD_W), lam, F32), subln_w.reshape(1, HEAD_W), q, k_new, v_new,
      *([cache_k] * pages), *([cache_v] * pages))


def _tmlp_kernel(zu_ref, zv_ref, lnw_ref, lnb_ref, wm_ref, bias_ref, oc_ref, *vc_out,
                 mix_rows, period, n_sub):
    u = _gelu(zu_ref[...].astype(F32))
    g = _gelu(zv_ref[...].astype(F32))
    xc = g - jnp.mean(g, axis=-1, keepdims=True)
    vc = xc * lax.rsqrt(jnp.mean(xc * xc, axis=-1, keepdims=True) + EPS) * lnw_ref[...] + lnb_ref[...]
    if vc_out:
        vc_out[0][...] = vc
    row = lax.broadcasted_iota(jnp.int32, (mix_rows, mix_rows), 0)
    col = lax.broadcasted_iota(jnp.int32, (mix_rows, mix_rows), 1)
    sh = _log2(period)
    allow = ((row >> sh) == (col >> sh)) & ((col & (period - 1)) <= (row & (period - 1)))
    for gi in range(N_HEADS):
        sl = slice(gi * HEAD_W, (gi + 1) * HEAD_W)
        wm = jnp.where(allow, wm_ref[gi], 0.0).astype(BF16)
        for sb in range(n_sub):
            rs = slice(sb * mix_rows, (sb + 1) * mix_rows)
            mixed = _dot(wm, vc[rs, sl].astype(BF16)) + bias_ref[:, sl]
            oc_ref[rs, sl] = (u[rs, sl] * mixed).astype(oc_ref.dtype)


def _token_mlp(z, ln_w, ln_b, w_s, b_s, tile_rows, mix_rows, period, emit_v, out_dtype):
    t = z.shape[0]
    reps = mix_rows // period
    wm = jnp.tile(w_s[:, :period, :period], (1, reps, reps))
    bias = jnp.tile(jnp.repeat(b_s[:, :period].T, HEAD_W, axis=1), (reps, 1))
    zspec = lambda cb: pl.BlockSpec((tile_rows, BRANCH_W), lambda i: (i, cb))
    vec = pl.BlockSpec((1, BRANCH_W), lambda i: (0, 0))
    ospec = pl.BlockSpec((tile_rows, BRANCH_W), lambda i: (i, 0))
    out_shape = [jax.ShapeDtypeStruct((t, BRANCH_W), out_dtype)]
    if emit_v:
        out_shape.append(jax.ShapeDtypeStruct((t, BRANCH_W), F32))
    return pl.pallas_call(
        functools.partial(_tmlp_kernel, mix_rows=mix_rows, period=period, n_sub=tile_rows // mix_rows),
        grid=(t // tile_rows,),
        in_specs=[zspec(COL_CU), zspec(COL_CV), vec, vec,
                  pl.BlockSpec((N_HEADS, mix_rows, mix_rows), lambda i: (0, 0, 0)),
                  pl.BlockSpec((mix_rows, BRANCH_W), lambda i: (0, 0))],
        out_specs=[ospec] * len(out_shape),
        out_shape=out_shape,
        compiler_params=pltpu.CompilerParams(dimension_semantics=("parallel",)),
        name="token_mlp",
    )(z, z, ln_w.reshape(1, BRANCH_W), ln_b.reshape(1, BRANCH_W), wm, bias)


def _merge_kernel(oa_ref, ob_ref, oc_ref, zg0_ref, zg1_ref, x_ref, wa_ref, wb_ref, wc_ref, wo_ref,
                  xo_ref):
    pa = _dot(oa_ref[...].astype(BF16), wa_ref[...])
    pb = _dot(ob_ref[...].astype(BF16), wb_ref[...])
    pc = _dot(oc_ref[...].astype(BF16), wc_ref[...])
    zg0 = zg0_ref[...].astype(F32)
    zg1 = zg1_ref[...].astype(F32)
    split = 2 * D_MODEL - GATE_BLOCK_W
    g_a = _sigmoid_gate(zg0[:, :D_MODEL])
    g_b = _sigmoid_gate(jnp.concatenate([zg0[:, D_MODEL:], zg1[:, :split]], axis=1))
    g_c = _sigmoid_gate(zg1[:, split:])
    merged = g_a * pa + g_b * pb + g_c * pc
    xo_ref[...] = x_ref[...] + _dot(merged.astype(BF16), wo_ref[...])


def _merge(oa, ob, oc, z, x, wa, wb, wc, wo, tm):
    t = x.shape[0]
    bspec = pl.BlockSpec((tm, BRANCH_W), lambda i: (i, 0))
    gspec = lambda cb: pl.BlockSpec((tm, GATE_BLOCK_W), lambda i: (i, cb))
    xspec = pl.BlockSpec((tm, D_MODEL), lambda i: (i, 0))
    wspec = pl.BlockSpec((BRANCH_W, D_MODEL), lambda i: (0, 0))
    return pl.pallas_call(
        _merge_kernel,
        grid=(t // tm,),
        in_specs=[bspec, bspec, bspec, gspec(3), gspec(4), xspec, wspec, wspec, wspec,
                  pl.BlockSpec((D_MODEL, D_MODEL), lambda i: (0, 0))],
        out_specs=xspec,
        out_shape=jax.ShapeDtypeStruct((t, D_MODEL), F32),
        compiler_params=pltpu.CompilerParams(dimension_semantics=("parallel",)),
        name="merge",
    )(oa, ob, oc, z, z, x, wa, wb, wc, wo)


def _ffn_kernel(x_ref, nw_ref, wa_ref, wb_ref, wd_ref, cw_ref, cb_ref, cbuf_ref, xo_ref, last_ref,
                h_s, acc_s, abuf, *, tm, shift, tiles_per_seq, pad):
    i = pl.program_id(0)
    j = pl.program_id(1)
    nff = pl.num_programs(1)
    halo = (CONV_W - 1) * shift

    @pl.when(j == 0)
    def _():
        x = x_ref[...]
        ms = jnp.mean(x * x, axis=-1, keepdims=True)
        h_s[...] = (x * lax.rsqrt(ms + EPS) * nw_ref[...]).astype(BF16)
        acc_s[...] = jnp.zeros(acc_s.shape, F32)

    @pl.when(i % tiles_per_seq == 0)
    def _():
        abuf[j, pad - halo:pad, :] = cbuf_ref[0]

    @pl.when(i % tiles_per_seq != 0)
    def _():
        abuf[j, pad - halo:pad, :] = abuf[j, pad + tm - halo:pad + tm, :]

    h = h_s[...]
    a = _dot(h, wa_ref[...])
    bb = _dot(h, wb_ref[...])
    abuf[j, pad:pad + tm, :] = a
    a1 = abuf[j, pad - shift:pad - shift + tm, :]
    a2 = abuf[j, pad - 2 * shift:pad - 2 * shift + tm, :]
    conv = cb_ref[...] + a2 * cw_ref[0:1, :] + a1 * cw_ref[1:2, :] + a * cw_ref[2:3, :]
    acc_s[...] += _dot((_gelu(conv) * bb).astype(BF16), wd_ref[...])
    last_ref[0] = a[tm - halo:, :]

    @pl.when(j == nff - 1)
    def _():
        xo_ref[...] = x_ref[...] + acc_s[...]


def _ffn(x, norm_w, w_up, w_down, conv_w, conv_b, conv_buf, tm, tff, shift, tiles_per_seq):
    t = x.shape[0]
    nff = D_FF // tff
    halo = (CONV_W - 1) * shift
    pad = -(-halo // SUBLANES) * SUBLANES
    n_tiles = t // tm
    return pl.pallas_call(
        functools.partial(_ffn_kernel, tm=tm, shift=shift, tiles_per_seq=tiles_per_seq, pad=pad),
        grid=(n_tiles, nff),
        in_specs=[
            pl.BlockSpec((tm, D_MODEL), lambda i, j: (i, 0)),
            pl.BlockSpec((1, D_MODEL), lambda i, j: (0, 0)),
            pl.BlockSpec((D_MODEL, tff), lambda i, j: (0, j)),
            pl.BlockSpec((D_MODEL, tff), lambda i, j: (0, nff + j)),
            pl.BlockSpec((tff, D_MODEL), lambda i, j: (j, 0)),
            pl.BlockSpec((CONV_W, tff), lambda i, j: (0, j)),
            pl.BlockSpec((1, tff), lambda i, j: (0, j)),
            pl.BlockSpec((1, halo, tff), lambda i, j: (i // tiles_per_seq, 0, j)),
        ],
        out_specs=[
            pl.BlockSpec((tm, D_MODEL), lambda i, j: (i, 0)),
            pl.BlockSpec((1, halo, tff), lambda i, j: (i, 0, j)),
        ],
        out_shape=[
            jax.ShapeDtypeStruct((t, D_MODEL), F32),
            jax.ShapeDtypeStruct((n_tiles, halo, D_FF), F32),
        ],
        scratch_shapes=[
            pltpu.VMEM((tm, D_MODEL), BF16),
            pltpu.VMEM((tm, D_MODEL), F32),
            pltpu.VMEM((nff, pad + tm, tff), F32),
        ],
        compiler_params=pltpu.CompilerParams(dimension_semantics=("arbitrary", "arbitrary")),
        name="conv_ffn",
    )(x, norm_w.reshape(1, D_MODEL), w_up, w_up, w_down, conv_w, conv_b.reshape(1, D_FF), conv_buf)


def _row_tile(t, want):
    return min(t, want)


def _layer(x, nseq, seq_len, layer, pos0, s0, conv_state, past, w, is_sample):
    t = nseq * seq_len
    tm = _row_tile(t, 512)
    act_dtype = F32 if is_sample else BF16
    z, zf = _inproj(x, w["norm_mix_w"], w["w_in"], _row_tile(t, 1024), 1536, act_dtype)

    chunk = min(A_CHUNK, seq_len)
    oa, s_new = _hgrn(z, zf, w["lb"], w["a_norm_w"], s0, nseq, seq_len, chunk,
                      math.gcd(seq_len // chunk, 8), act_dtype)

    tab_rows = max(seq_len, tm)
    tabs = _rope_tables(pos0, seq_len, tab_rows)
    qkv = _qkv_prep(z, tabs, w["q_norm_w"], w["k_norm_w"], tm, tab_rows // tm, not is_sample)
    lam_init = 0.8 - 0.6 * math.exp(-0.3 * layer)
    lam = (jnp.exp(jnp.sum(w["lambda_q1"] * w["lambda_k1"]))
           - jnp.exp(jnp.sum(w["lambda_q2"] * w["lambda_k2"])) + lam_init)
    if is_sample:
        q, k_rows, v_rows = qkv
        cache_k, cache_v, page_table = past
        ob = _sample_attn(q, k_rows, v_rows, cache_k, cache_v, page_table, layer, lam, w["b_subln_w"],
                          nseq, seq_len, math.gcd(page_table.shape[1], 16), 1.0 - lam_init)
    else:
        q, k_rows, v_rows, k_bf, v_bf = qkv
        ob = _prompt_attn(q, k_bf, v_bf, lam, w["b_subln_w"], nseq, seq_len, min(seq_len, 512),
                          min(seq_len, 512), 1.0 - lam_init, act_dtype)

    if is_sample:
        oc, vc = _token_mlp(z, w["c_ln_w"], w["c_ln_b"], w["c_w_s"], w["c_b_s"], t, t, seq_len, True,
                            act_dtype)
    else:
        (oc,) = _token_mlp(z, w["c_ln_w"], w["c_ln_b"], w["c_w_s"], w["c_b_s"], tm, C_CHUNK, C_CHUNK,
                           False, act_dtype)
        vc = None

    x_mid = _merge(oa, ob, oc, z, x, w["w_branch_a"], w["w_branch_b"], w["w_branch_c"], w["w_out"], tm)

    if is_sample:
        xt = x_mid.reshape(nseq, seq_len, D_MODEL).transpose(1, 0, 2).reshape(t, D_MODEL)
        cbuf = conv_state.transpose(1, 0, 2).reshape(1, (CONV_W - 1) * nseq, D_FF)
        xo, last = _ffn(xt, w["norm_ffn_w"], w["w_up"], w["w_down"], w["conv_w"], w["conv_b"], cbuf,
                        t, 1408, nseq, 1)
        x_new = xo.reshape(seq_len, nseq, D_MODEL).transpose(1, 0, 2).reshape(t, D_MODEL)
        new_conv = last.reshape(CONV_W - 1, nseq, D_FF).transpose(1, 0, 2)
    else:
        tiles_per_seq = seq_len // tm
        x_new, last = _ffn(x_mid, w["norm_ffn_w"], w["w_up"], w["w_down"], w["conv_w"], w["conv_b"],
                           conv_state, tm, 1408, 1, tiles_per_seq)
        new_conv = last.reshape(nseq, tiles_per_seq, CONV_W - 1, D_FF)[:, -1]
    return x_new, s_new, k_rows, v_rows, vc, new_conv


def kernel(x_prompt, x_sample, cache_k, cache_v, page_table, state_hgrn, state_conv,
           norm_mix_w, w_in, lb_logits, a_norm_w, q_norm_w, k_norm_w,
           lambda_q1, lambda_k1, lambda_q2, lambda_k2, b_subln_w,
           c_ln_w, c_ln_b, c_w_s, c_b_s, w_branch_a, w_branch_b, w_branch_c, w_out,
           norm_ffn_w, w_up, conv_w, conv_b, w_down):
    depth = w_in.shape[0]
    n_p, len_p, _ = x_prompt.shape
    n_s, len_s, _ = x_sample.shape
    past_len = page_table.shape[1] * PAGE_SIZE
    p_lb = jax.nn.softmax(lb_logits.astype(F32), axis=0)
    lower_bounds = jnp.cumsum(p_lb, axis=0) - p_lb[0:1]

    xp = x_prompt.reshape(n_p * len_p, D_MODEL)
    xs = x_sample.reshape(n_s * len_s, D_MODEL)
    outs_p, outs_s = [], []
    for l in range(depth):
        w = dict(norm_mix_w=norm_mix_w[l], w_in=w_in[l].astype(BF16), lb=lower_bounds[l],
                 a_norm_w=a_norm_w[l], q_norm_w=q_norm_w[l], k_norm_w=k_norm_w[l],
                 lambda_q1=lambda_q1[l], lambda_k1=lambda_k1[l],
                 lambda_q2=lambda_q2[l], lambda_k2=lambda_k2[l], b_subln_w=b_subln_w[l],
                 c_ln_w=c_ln_w[l], c_ln_b=c_ln_b[l], c_w_s=c_w_s[l], c_b_s=c_b_s[l],
                 w_branch_a=w_branch_a[l].astype(BF16), w_branch_b=w_branch_b[l].astype(BF16),
                 w_branch_c=w_branch_c[l].astype(BF16), w_out=w_out[l].astype(BF16),
                 norm_ffn_w=norm_ffn_w[l], w_up=w_up[l].astype(BF16), conv_w=conv_w[l],
                 conv_b=conv_b[l], w_down=w_down[l].astype(BF16))
        s0_p = jnp.zeros((n_p, N_HEADS, HEAD_W, HEAD_W), F32)
        conv0_p = jnp.zeros((n_p, CONV_W - 1, D_FF), F32)
        xp, *rest_p = _layer(xp, n_p, len_p, l, 0, s0_p, conv0_p, None, w, False)
        outs_p.append(rest_p)
        xs, *rest_s = _layer(xs, n_s, len_s, l, past_len, state_hgrn[l], state_conv[l],
                             (cache_k, cache_v, page_table), w, True)
        outs_s.append(rest_s)

    def stack(outs, idx, shape):
        return jnp.stack([o[idx].reshape(shape) for o in outs])

    kv_p = (n_p, len_p, N_HEADS, HEAD_W)
    kv_s = (n_s, len_s, N_HEADS, HEAD_W)
    return (xp.reshape(n_p, len_p, D_MODEL), xs.reshape(n_s, len_s, D_MODEL),
            stack(outs_p, 1, kv_p), stack(outs_p, 2, kv_p),
            stack(outs_p, 0, (n_p, N_HEADS, HEAD_W, HEAD_W)),
            stack(outs_p, 4, (n_p, CONV_W - 1, D_FF)),
            stack(outs_s, 1, kv_s), stack(outs_s, 2, kv_s),
            stack(outs_s, 0, (n_s, N_HEADS, HEAD_W, HEAD_W)),
            stack(outs_s, 4, (n_s, CONV_W - 1, D_FF)),
            stack(outs_s, 3, (n_s, len_s, BRANCH_W)))
```

```python
import functools
import math

import jax
import jax.numpy as jnp
from jax import lax
from jax.experimental import pallas as pl
from jax.experimental.pallas import tpu as pltpu

F32 = jnp.float32
BF16 = jnp.bfloat16

D_MODEL = 1024
BRANCH_W = 512
N_HEADS = 4
HEAD_W = 128
QK_DH = 64
ROPE_THETA = 10000.0
A_CHUNK = 64
C_CHUNK = 128
PAGE_SIZE = 128
D_FF = 2816
CONV_W = 3
IN_COLS = 9 * BRANCH_W + 3 * D_MODEL
EPS = 1e-6
NEG_INF = -1e30
LANES = 128
SUBLANES = 8

COL_AQ, COL_AF, COL_AI, COL_AG, COL_BQ, COL_BK, COL_BV, COL_CU, COL_CV = range(9)
GATE_BLOCK_W = 1536
MXU_TILE = 256
FFN_COL_CHUNK = 6 * MXU_TILE


def _sigmoid(x):
    return 1.0 / (1.0 + jnp.exp(-x))


def _sigmoid_gate(x):
    return 0.5 * jnp.tanh(0.5 * x) + 0.5


def _gelu(x):
    c = math.sqrt(2.0 / math.pi)
    return 0.5 * x * (1.0 + jnp.tanh(c * (x + 0.044715 * (x * x * x))))


def _dot(a, b):
    return jnp.dot(a, b, preferred_element_type=F32)


def _dot_nt(a, b):
    return lax.dot_general(a, b, (((1,), (1,)), ((), ())), preferred_element_type=F32)


def _dot_tn(a, b):
    return lax.dot_general(a, b, (((0,), (0,)), ((), ())), preferred_element_type=F32)


def _cumsum_rows(x):
    row = lax.broadcasted_iota(jnp.int32, x.shape, 0)
    d = 1
    while d < x.shape[0]:
        x = x + jnp.where(row >= d, pltpu.roll(x, d, 0), 0.0)
        d *= 2
    return x


def _pad_rows(a, rows):
    if a.shape[0] >= rows:
        return a
    return jnp.concatenate([a, jnp.zeros((rows - a.shape[0],) + a.shape[1:], a.dtype)], axis=0)


def _log2(n):
    assert n & (n - 1) == 0, n
    return n.bit_length() - 1


def _inproj_kernel(x_ref, nw_ref, w_ref, z_ref, zf_ref, h_ref):
    j = pl.program_id(1)

    @pl.when(j == 0)
    def _():
        x = x_ref[...]
        ms = jnp.mean(x * x, axis=-1, keepdims=True)
        h_ref[...] = (x * lax.rsqrt(ms + EPS) * nw_ref[...]).astype(BF16)

    z = _dot(h_ref[...], w_ref[...])
    z_ref[...] = z.astype(z_ref.dtype)

    @pl.when(j == 0)
    def _():
        zf_ref[...] = z[:, COL_AF * BRANCH_W:(COL_AF + 1) * BRANCH_W]


def _inproj(x, norm_w, w_bf16, tm, tn, z_dtype):
    t = x.shape[0]
    assert tn >= (COL_AF + 1) * BRANCH_W
    return pl.pallas_call(
        _inproj_kernel,
        grid=(t // tm, IN_COLS // tn),
        in_specs=[
            pl.BlockSpec((tm, D_MODEL), lambda i, j: (i, 0)),
            pl.BlockSpec((1, D_MODEL), lambda i, j: (0, 0)),
            pl.BlockSpec((D_MODEL, tn), lambda i, j: (0, j)),
        ],
        out_specs=[pl.BlockSpec((tm, tn), lambda i, j: (i, j)),
                   pl.BlockSpec((tm, BRANCH_W), lambda i, j: (i, 0))],
        out_shape=[jax.ShapeDtypeStruct((t, IN_COLS), z_dtype),
                   jax.ShapeDtypeStruct((t, BRANCH_W), F32)],
        scratch_shapes=[pltpu.VMEM((tm, D_MODEL), BF16)],
        compiler_params=pltpu.CompilerParams(dimension_semantics=("parallel", "arbitrary")),
        name="inproj",
    )(x, norm_w.reshape(1, D_MODEL), w_bf16)


def _hgrn_kernel(zq_ref, zf_ref, zi_ref, zg_ref, lb_ref, nw_ref, s0_ref, oa_ref, sfin_ref,
                 st_ref, c_s, b_s, *, chunk, blk, n_sub):
    c = pl.program_id(1)
    nc = pl.num_programs(1)

    @pl.when(c == 0)
    def _():
        for h in range(N_HEADS):
            st_ref[h] = s0_ref[0, h].T

    n_blk = chunk // blk
    t_loc = lax.broadcasted_iota(jnp.int32, (blk, HEAD_W), 0)
    lane = lax.broadcasted_iota(jnp.int32, (blk, LANES), 1)
    ones = jnp.ones((HEAD_W, LANES), BF16)
    mm_rows = max(chunk, 2 * SUBLANES)

    def one_chunk(rows):
        lb = lb_ref[...]
        sig = _sigmoid(zf_ref[rows, :])
        k_all = (1.0 - lb) * (1.0 - sig)
        b_all = _cumsum_rows(jnp.log(lb + (1.0 - lb) * sig))
        c_all = jnp.log(k_all) - b_all
        for h in range(N_HEADS):
            sl = slice(h * HEAD_W, (h + 1) * HEAD_W)
            k = k_all[:, sl]
            b = b_all[:, sl]
            zq = zq_ref[rows, sl].astype(F32)
            q = zq * _sigmoid_gate(zq)
            v = zi_ref[rows, sl].astype(F32)
            c_s[h] = c_all[:, sl]
            b_s[h] = b

            row_blocks = []
            for i in range(n_blk):
                r0 = i * blk
                q_i = q[r0:r0 + blk]
                b_i = b[r0:r0 + blk]
                terms, starts, n_rows = [], [], 0
                for s in range(blk):
                    g0 = (s // SUBLANES) * SUBLANES
                    cs = c_s[h, r0 + s:r0 + s + 1, :]
                    e = jnp.exp(jnp.where(t_loc[g0:] >= s, b_i[g0:] + cs, NEG_INF))
                    terms.append(q_i[g0:] * e)
                    starts.append(n_rows - g0)
                    n_rows += blk - g0
                sums = _dot(jnp.concatenate(terms, axis=0).astype(BF16), ones)
                groups = []
                for g in range(0, blk, SUBLANES):
                    sc_g = jnp.zeros((SUBLANES, LANES), F32)
                    for s in range(min(blk, g + SUBLANES)):
                        rows_s = sums[starts[s] + g:starts[s] + g + SUBLANES]
                        sc_g = jnp.where(lane[:SUBLANES] == r0 + s, rows_s, sc_g)
                    groups.append(sc_g)
                sc_i = jnp.concatenate(groups, axis=0)
                if i > 0:
                    b0 = b_s[h, r0:r0 + 1, :]
                    a_i = _pad_rows(q_i * jnp.exp(b_i - b0), 2 * SUBLANES).astype(BF16)
                    k_left = _pad_rows(k[:r0] * jnp.exp(b0 - b[:r0]), LANES).astype(BF16)
                    sc_i = sc_i + _dot_nt(a_i, k_left)[:blk]
                row_blocks.append(sc_i)
            scores = jnp.concatenate(row_blocks, axis=0)

            st = st_ref[h]
            v_pad = _pad_rows(v, LANES).astype(BF16)
            qd = _pad_rows(q * jnp.exp(b), mm_rows).astype(BF16)
            sc_bf = _pad_rows(scores, mm_rows).astype(BF16)
            o = (_dot_nt(qd, st.astype(BF16)) + _dot(sc_bf, v_pad))[:chunk]

            b_last = b[chunk - 1:chunk, :]
            kd = _pad_rows(k * jnp.exp(b_last - b), LANES).astype(BF16)
            st_ref[h] = jnp.exp(b_last) * st + _dot_tn(v_pad, kd)

            ms = jnp.mean(o * o, axis=-1, keepdims=True)
            on = o * lax.rsqrt(ms + EPS) * nw_ref[...]
            oa_ref[rows, sl] = (on * _sigmoid_gate(zg_ref[rows, sl].astype(F32))).astype(oa_ref.dtype)

    if n_sub == 1:
        one_chunk(pl.ds(0, chunk))
    else:
        def body(ci, carry):
            one_chunk(pl.ds(pl.multiple_of(ci * chunk, chunk), chunk))
            return carry

        lax.fori_loop(0, n_sub, body, 0, unroll=4)

    @pl.when(c == nc - 1)
    def _():
        for h in range(N_HEADS):
            sfin_ref[0, h] = st_ref[h].T


def _hgrn(z, zf, lb, norm_w, s0, nseq, seq_len, chunk, n_sub, out_dtype):
    rows = chunk * n_sub
    nc = seq_len // rows
    zspec = lambda cb: pl.BlockSpec((rows, BRANCH_W), lambda b, c: (b * nc + c, cb))
    return pl.pallas_call(
        functools.partial(_hgrn_kernel, chunk=chunk, blk=min(chunk, 2 * SUBLANES), n_sub=n_sub),
        grid=(nseq, nc),
        in_specs=[
            zspec(COL_AQ), zspec(0), zspec(COL_AI), zspec(COL_AG),
            pl.BlockSpec((1, BRANCH_W), lambda b, c: (0, 0)),
            pl.BlockSpec((1, HEAD_W), lambda b, c: (0, 0)),
            pl.BlockSpec((1, N_HEADS, HEAD_W, HEAD_W), lambda b, c: (b, 0, 0, 0)),
        ],
        out_specs=[
            pl.BlockSpec((rows, BRANCH_W), lambda b, c: (b * nc + c, 0)),
            pl.BlockSpec((1, N_HEADS, HEAD_W, HEAD_W), lambda b, c: (b, 0, 0, 0)),
        ],
        out_shape=[
            jax.ShapeDtypeStruct((nseq * seq_len, BRANCH_W), out_dtype),
            jax.ShapeDtypeStruct((nseq, N_HEADS, HEAD_W, HEAD_W), F32),
        ],
        scratch_shapes=[
            pltpu.VMEM((N_HEADS, HEAD_W, HEAD_W), F32),
            pltpu.VMEM((N_HEADS, chunk, HEAD_W), F32),
            pltpu.VMEM((N_HEADS, chunk, HEAD_W), F32),
        ],
        compiler_params=pltpu.CompilerParams(dimension_semantics=("parallel", "arbitrary")),
        name="hgrn",
    )(z, zf, z, z, lb.reshape(1, BRANCH_W), norm_w.reshape(1, HEAD_W), s0)


def _qkv_kernel(zq_ref, zk_ref, zv_ref, cos_ref, sa_ref, sb_ref, qw_ref, kw_ref, g_ref,
                q_out, kf_out, vf_out, *bf_outs):
    cos = cos_ref[...]
    sa = sa_ref[...]
    sb = sb_ref[...]
    gmat = g_ref[...]

    def norm_rope(t, w):
        sq = t * t
        hi = sq.astype(BF16)
        lo = (sq - hi.astype(F32)).astype(BF16)
        ss = _dot(hi, gmat) + _dot(lo, gmat)
        y = t * lax.rsqrt(ss * (1.0 / QK_DH) + EPS) * w
        return (y * cos + pltpu.roll(y, LANES - QK_DH // 2, 1) * sa
                + pltpu.roll(y, QK_DH // 2, 1) * sb)

    for h in range(N_HEADS):
        sl = slice(h * HEAD_W, (h + 1) * HEAD_W)
        qr = norm_rope(zq_ref[:, sl].astype(F32), qw_ref[...])
        kr = norm_rope(zk_ref[:, sl].astype(F32), kw_ref[...])
        q_out[:, sl] = (qr * (QK_DH ** -0.5)).astype(q_out.dtype)
        rows = pl.ds(h, kr.shape[0], stride=N_HEADS)
        kf_out[rows, :] = kr
        v = zv_ref[:, sl].astype(F32)
        vf_out[rows, :] = v
        if bf_outs:
            bf_outs[0][:, sl] = kr.astype(BF16)
            bf_outs[1][:, sl] = v.astype(BF16)


def _qkv_prep(z, tabs, q_norm_w, k_norm_w, tm, tab_blocks, emit_bf16):
    t = z.shape[0]
    zspec = lambda cb: pl.BlockSpec((tm, BRANCH_W), lambda i: (i, cb))
    tspec = pl.BlockSpec((tm, LANES), lambda i: (i % tab_blocks, 0))
    wspec = pl.BlockSpec((1, LANES), lambda i: (0, 0))
    ospec = pl.BlockSpec((tm, BRANCH_W), lambda i: (i, 0))
    rspec = pl.BlockSpec((tm * N_HEADS, HEAD_W), lambda i: (i, 0))
    gi = lax.broadcasted_iota(jnp.int32, (LANES, LANES), 0) // QK_DH
    gj = lax.broadcasted_iota(jnp.int32, (LANES, LANES), 1) // QK_DH
    gmat = (gi == gj).astype(BF16)
    n_out = 5 if emit_bf16 else 3
    dts = [BF16 if emit_bf16 else F32, F32, F32, BF16, BF16][:n_out]
    return pl.pallas_call(
        _qkv_kernel,
        grid=(t // tm,),
        in_specs=[zspec(COL_BQ), zspec(COL_BK), zspec(COL_BV), tspec, tspec, tspec, wspec, wspec,
                  pl.BlockSpec((LANES, LANES), lambda i: (0, 0))],
        out_specs=[ospec, rspec, rspec] + [ospec] * (n_out - 3),
        out_shape=[jax.ShapeDtypeStruct((t * N_HEADS, HEAD_W) if i in (1, 2) else (t, BRANCH_W), d)
                   for i, d in enumerate(dts)],
        compiler_params=pltpu.CompilerParams(dimension_semantics=("parallel",)),
        name="qkv_prep",
    )(z, z, z, *tabs, jnp.tile(q_norm_w, 2).reshape(1, LANES),
      jnp.tile(k_norm_w, 2).reshape(1, LANES), gmat)


def _rope_tables(pos0, seq_len, rows):
    half = QK_DH // 2
    inv = ROPE_THETA ** (-jnp.arange(half, dtype=F32) / half)
    pos = (pos0 + jnp.arange(seq_len, dtype=jnp.int32)).astype(F32)
    ang = pos[:, None] * inv[None, :]
    cos, sin = jnp.cos(ang), jnp.sin(ang)
    zero = jnp.zeros_like(sin)
    per_head = lambda a, b: jnp.tile(jnp.concatenate([a, b], axis=-1), (rows // seq_len, LANES // QK_DH))
    return per_head(cos, cos), per_head(-sin, zero), per_head(zero, sin)


def _softmax_step(s, v_bf16, m_ref, l_ref, acc_ref):
    m_old = m_ref[...]
    m_new = jnp.maximum(m_old, jnp.max(s, axis=-1, keepdims=True))
    alpha = jnp.exp(m_old - m_new)
    p = jnp.exp((s - jnp.concatenate([m_new] * (s.shape[1] // LANES), axis=1)).astype(BF16))
    l_ref[...] = alpha * l_ref[...] + jnp.sum(p.astype(F32), axis=-1, keepdims=True)
    acc_ref[...] = alpha * acc_ref[...] + _dot(p, v_bf16)
    m_ref[...] = m_new


def _subln(o, w, scale):
    ms = jnp.mean(o * o, axis=-1, keepdims=True)
    return o * lax.rsqrt(ms + EPS) * w * scale


def _attn_kernel(lam_ref, w_ref, q_ref, k_ref, v_ref, o_ref, q1_s, q2_s, m1, l1, a1, m2, l2, a2,
                 *, tq, tk, out_scale):
    nq = q_ref.shape[0] // tq
    kpq = tq // tk

    def q_rows(i):
        return pl.ds(pl.multiple_of(i * tq, tq), tq)

    def update(qi, ki, masked):
        k_rows = pl.ds(pl.multiple_of(ki * tk, tk), tk)
        k = k_ref[k_rows, :]
        v = v_ref[k_rows, :]
        s1 = _dot_nt(q1_s[...], k)
        s2 = _dot_nt(q2_s[...], k)
        if masked:
            ahead = (lax.broadcasted_iota(jnp.int32, s1.shape, 1)
                     - lax.broadcasted_iota(jnp.int32, s1.shape, 0))
            visible = ahead <= qi * tq - ki * tk
            s1 = jnp.where(visible, s1, NEG_INF)
            s2 = jnp.where(visible, s2, NEG_INF)
        _softmax_step(s1, v, m1, l1, a1)
        _softmax_step(s2, v, m2, l2, a2)

    def query_block(qi, carry):
        q = q_ref[q_rows(qi), :]
        lane = lax.broadcasted_iota(jnp.int32, q.shape, 1)
        zero = jnp.zeros_like(q)
        q1_s[...] = jnp.where(lane < QK_DH, q, zero)
        q2_s[...] = jnp.where(lane >= QK_DH, q, zero)
        for m, l, a in ((m1, l1, a1), (m2, l2, a2)):
            m[...] = jnp.full(m.shape, NEG_INF, F32)
            l[...] = jnp.zeros(l.shape, F32)
            a[...] = jnp.zeros(a.shape, F32)

        def key_block_pair(j, c):
            update(qi, 2 * j, False)
            update(qi, 2 * j + 1, False)
            return c

        n_past = qi * kpq
        lax.fori_loop(0, n_past // 2, key_block_pair, 0)

        @pl.when(n_past % 2 == 1)
        def _():
            update(qi, n_past - 1, False)

        for d in range(kpq):
            update(qi, qi * kpq + d, True)
        o = a1[...] / l1[...] - lam_ref[...] * (a2[...] / l2[...])
        o_ref[q_rows(qi), :] = _subln(o, w_ref[...], out_scale).astype(o_ref.dtype)
        return carry

    lax.fori_loop(0, nq, query_block, 0)


def _prompt_attn(q, k, v, lam, subln_w, nseq, seq_len, tq, tk, out_scale, out_dtype):
    sspec = pl.BlockSpec((seq_len, HEAD_W), lambda b, h: (b, h))
    cspec = pl.BlockSpec((1, HEAD_W), lambda b, h: (0, 0))
    qbuf = pltpu.VMEM((tq, HEAD_W), q.dtype)
    stat = pltpu.VMEM((tq, LANES), F32)
    acc = pltpu.VMEM((tq, HEAD_W), F32)
    return pl.pallas_call(
        functools.partial(_attn_kernel, tq=tq, tk=tk, out_scale=out_scale),
        grid=(nseq, N_HEADS),
        in_specs=[cspec, cspec, sspec, sspec, sspec],
        out_specs=sspec,
        out_shape=jax.ShapeDtypeStruct((nseq * seq_len, BRANCH_W), out_dtype),
        scratch_shapes=[qbuf, qbuf, stat, stat, acc, stat, stat, acc],
        compiler_params=pltpu.CompilerParams(dimension_semantics=("parallel", "parallel")),
        name="prompt_attn",
    )(jnp.full((1, HEAD_W), lam, F32), subln_w.reshape(1, HEAD_W), q, k, v)


def _sattn_kernel(pt_ref, lam_ref, w_ref, q_ref, kn_ref, vn_ref, *rest, pages, dec_seq, out_scale):
    del pt_ref
    k_refs = rest[:pages]
    v_refs = rest[pages:2 * pages]
    o_ref = rest[2 * pages]
    qall, m_ref, l_ref, acc_ref = rest[2 * pages + 1:]
    g = pl.program_id(1)
    ng = pl.num_programs(1)
    hr = 2 * dec_seq

    @pl.when(g == 0)
    def _():
        lane = lax.broadcasted_iota(jnp.int32, (dec_seq, HEAD_W), 1)
        parts = []
        for h in range(N_HEADS):
            qh = q_ref[:, h * HEAD_W:(h + 1) * HEAD_W]
            parts += [jnp.where(lane < QK_DH, qh, 0.0), jnp.where(lane >= QK_DH, qh, 0.0)]
        qall[...] = jnp.concatenate(parts, axis=0).astype(BF16)
        m_ref[...] = jnp.full(m_ref.shape, NEG_INF, F32)
        l_ref[...] = jnp.zeros(l_ref.shape, F32)
        acc_ref[...] = jnp.zeros(acc_ref.shape, F32)

    def step(k_of_head, v_of_head, mask):
        s = jnp.concatenate([_dot_nt(qall[h * hr:(h + 1) * hr, :], k_of_head(h))
                             for h in range(N_HEADS)], axis=0)
        if mask is not None:
            s = jnp.where(mask(s.shape), s, NEG_INF)
        m_old = m_ref[...]
        m_new = jnp.maximum(m_old, jnp.max(s, axis=-1, keepdims=True))
        alpha = jnp.exp(m_old - m_new)
        p = jnp.exp(s - m_new)
        l_ref[...] = alpha * l_ref[...] + jnp.sum(p, axis=-1, keepdims=True)
        pv = jnp.concatenate([_dot(p[h * hr:(h + 1) * hr].astype(BF16), v_of_head(h))
                              for h in range(N_HEADS)], axis=0)
        acc_ref[...] = alpha * acc_ref[...] + pv
        m_ref[...] = m_new

    def paged(refs):
        return lambda h: jnp.concatenate(
            [r[pl.ds(h, PAGE_SIZE, stride=N_HEADS), :] for r in refs], axis=0).astype(BF16)

    step(paged(k_refs), paged(v_refs), None)

    @pl.when(g == ng - 1)
    def _():
        def fresh(ref):
            return lambda h: _pad_rows(ref[pl.ds(h, dec_seq, stride=N_HEADS), :], LANES).astype(BF16)

        def causal(shape):
            tok = lax.broadcasted_iota(jnp.int32, shape, 0) & (dec_seq - 1)
            return lax.broadcasted_iota(jnp.int32, shape, 1) <= tok

        step(fresh(kn_ref), fresh(vn_ref), causal)
        acc = acc_ref[...]
        l = l_ref[...]
        for h in range(N_HEADS):
            r1 = slice(h * hr, h * hr + dec_seq)
            r2 = slice(h * hr + dec_seq, (h + 1) * hr)
            o = acc[r1] / l[r1] - lam_ref[...] * (acc[r2] / l[r2])
            o_ref[:, h * HEAD_W:(h + 1) * HEAD_W] = _subln(o, w_ref[...], out_scale).astype(o_ref.dtype)


def _sample_attn(q, k_new, v_new, cache_k, cache_v, page_table, layer, lam, subln_w, nseq, dec_seq,
                 pages, out_scale):
    n_pages = page_table.shape[1]
    rows = 2 * N_HEADS * dec_seq
    depth, n_pool = cache_k.shape[:2]
    cache_k = cache_k.reshape(depth, n_pool, PAGE_SIZE * N_HEADS, HEAD_W)
    cache_v = cache_v.reshape(depth, n_pool, PAGE_SIZE * N_HEADS, HEAD_W)

    def page_spec(i):
        return pl.BlockSpec((None, None, PAGE_SIZE * N_HEADS, HEAD_W),
                            lambda b, g, pt: (layer, pt[b, g * pages + i], 0, 0))

    cspec = pl.BlockSpec((1, HEAD_W), lambda b, g, pt: (0, 0))
    rspec = pl.BlockSpec((dec_seq, BRANCH_W), lambda b, g, pt: (b, 0))
    nspec = pl.BlockSpec((dec_seq * N_HEADS, HEAD_W), lambda b, g, pt: (b, 0))
    grid_spec = pltpu.PrefetchScalarGridSpec(
        num_scalar_prefetch=1,
        grid=(nseq, n_pages // pages),
        in_specs=[cspec, cspec, rspec, nspec, nspec]
        + [page_spec(i) for i in range(pages)] * 2,
        out_specs=rspec,
        scratch_shapes=[
            pltpu.VMEM((rows, HEAD_W), BF16),
            pltpu.VMEM((rows, 1), F32),
            pltpu.VMEM((rows, 1), F32),
            pltpu.VMEM((rows, HEAD_W), F32),
        ],
    )
    return pl.pallas_call(
        functools.partial(_sattn_kernel, pages=pages, dec_seq=dec_seq, out_scale=out_scale),
        grid_spec=grid_spec,
        out_shape=jax.ShapeDtypeStruct((nseq * dec_seq, BRANCH_W), F32),
        compiler_params=pltpu.CompilerParams(dimension_semantics=("parallel", "arbitrary")),
        name="sample_attn",
    )(page_table, jnp.full((1, HEAD_W), lam, F32), subln_w.reshape(1, HEAD_W), q, k_new, v_new,
      *([cache_k] * pages), *([cache_v] * pages))


def _tmlp_kernel(zu_ref, zv_ref, lnw_ref, lnb_ref, wm_ref, bias_ref, oc_ref, *vc_out,
                 mix_rows, period, n_sub):
    u = _gelu(zu_ref[...].astype(F32))
    g = _gelu(zv_ref[...].astype(F32))
    xc = g - jnp.mean(g, axis=-1, keepdims=True)
    vc = xc * lax.rsqrt(jnp.mean(xc * xc, axis=-1, keepdims=True) + EPS) * lnw_ref[...] + lnb_ref[...]
    if vc_out:
        vc_out[0][...] = vc
    row = lax.broadcasted_iota(jnp.int32, (mix_rows, mix_rows), 0)
    col = lax.broadcasted_iota(jnp.int32, (mix_rows, mix_rows), 1)
    sh = _log2(period)
    allow = ((row >> sh) == (col >> sh)) & ((col & (period - 1)) <= (row & (period - 1)))
    for gi in range(N_HEADS):
        sl = slice(gi * HEAD_W, (gi + 1) * HEAD_W)
        wm = jnp.where(allow, wm_ref[gi], 0.0).astype(BF16)
        for sb in range(n_sub):
            rs = slice(sb * mix_rows, (sb + 1) * mix_rows)
            mixed = _dot(wm, vc[rs, sl].astype(BF16)) + bias_ref[:, sl]
            oc_ref[rs, sl] = (u[rs, sl] * mixed).astype(oc_ref.dtype)


def _token_mlp(z, ln_w, ln_b, w_s, b_s, tile_rows, mix_rows, period, emit_v, out_dtype):
    t = z.shape[0]
    reps = mix_rows // period
    wm = jnp.tile(w_s[:, :period, :period], (1, reps, reps))
    bias = jnp.tile(jnp.repeat(b_s[:, :period].T, HEAD_W, axis=1), (reps, 1))
    zspec = lambda cb: pl.BlockSpec((tile_rows, BRANCH_W), lambda i: (i, cb))
    vec = pl.BlockSpec((1, BRANCH_W), lambda i: (0, 0))
    ospec = pl.BlockSpec((tile_rows, BRANCH_W), lambda i: (i, 0))
    out_shape = [jax.ShapeDtypeStruct((t, BRANCH_W), out_dtype)]
    if emit_v:
        out_shape.append(jax.ShapeDtypeStruct((t, BRANCH_W), F32))
    return pl.pallas_call(
        functools.partial(_tmlp_kernel, mix_rows=mix_rows, period=period, n_sub=tile_rows // mix_rows),
        grid=(t // tile_rows,),
        in_specs=[zspec(COL_CU), zspec(COL_CV), vec, vec,
                  pl.BlockSpec((N_HEADS, mix_rows, mix_rows), lambda i: (0, 0, 0)),
                  pl.BlockSpec((mix_rows, BRANCH_W), lambda i: (0, 0))],
        out_specs=[ospec] * len(out_shape),
        out_shape=out_shape,
        compiler_params=pltpu.CompilerParams(dimension_semantics=("parallel",)),
        name="token_mlp",
    )(z, z, ln_w.reshape(1, BRANCH_W), ln_b.reshape(1, BRANCH_W), wm, bias)


def _merge_kernel(oa_ref, ob_ref, oc_ref, zg0_ref, zg1_ref, x_ref, wa_ref, wb_ref, wc_ref, wo_ref,
                  xo_ref):
    pa = _dot(oa_ref[...].astype(BF16), wa_ref[...])
    pb = _dot(ob_ref[...].astype(BF16), wb_ref[...])
    pc = _dot(oc_ref[...].astype(BF16), wc_ref[...])
    zg0 = zg0_ref[...].astype(F32)
    zg1 = zg1_ref[...].astype(F32)
    split = 2 * D_MODEL - GATE_BLOCK_W
    g_a = _sigmoid_gate(zg0[:, :D_MODEL])
    g_b = _sigmoid_gate(jnp.concatenate([zg0[:, D_MODEL:], zg1[:, :split]], axis=1))
    g_c = _sigmoid_gate(zg1[:, split:])
    merged = g_a * pa + g_b * pb + g_c * pc
    xo_ref[...] = x_ref[...] + _dot(merged.astype(BF16), wo_ref[...])


def _merge(oa, ob, oc, z, x, wa, wb, wc, wo, tm):
    t = x.shape[0]
    bspec = pl.BlockSpec((tm, BRANCH_W), lambda i: (i, 0))
    gspec = lambda cb: pl.BlockSpec((tm, GATE_BLOCK_W), lambda i: (i, cb))
    xspec = pl.BlockSpec((tm, D_MODEL), lambda i: (i, 0))
    wspec = pl.BlockSpec((BRANCH_W, D_MODEL), lambda i: (0, 0))
    return pl.pallas_call(
        _merge_kernel,
        grid=(t // tm,),
        in_specs=[bspec, bspec, bspec, gspec(3), gspec(4), xspec, wspec, wspec, wspec,
                  pl.BlockSpec((D_MODEL, D_MODEL), lambda i: (0, 0))],
        out_specs=xspec,
        out_shape=jax.ShapeDtypeStruct((t, D_MODEL), F32),
        compiler_params=pltpu.CompilerParams(dimension_semantics=("parallel",)),
        name="merge",
    )(oa, ob, oc, z, z, x, wa, wb, wc, wo)


def _ffn_kernel(x_ref, nw_ref, wa_ref, wb_ref, wd_ref, cw_ref, cb_ref, cbuf_ref, xo_ref, last_ref,
                h_s, acc_s, abuf, *, tm, shift, tiles_per_seq, pad):
    i = pl.program_id(0)
    j = pl.program_id(1)
    nff = pl.num_programs(1)
    halo = (CONV_W - 1) * shift

    @pl.when(j == 0)
    def _():
        x = x_ref[...]
        ms = jnp.mean(x * x, axis=-1, keepdims=True)
        h_s[...] = (x * lax.rsqrt(ms + EPS) * nw_ref[...]).astype(BF16)
        acc_s[...] = jnp.zeros(acc_s.shape, F32)

    @pl.when(i % tiles_per_seq == 0)
    def _():
        abuf[j, pad - halo:pad, :] = cbuf_ref[0]

    @pl.when(i % tiles_per_seq != 0)
    def _():
        abuf[j, pad - halo:pad, :] = abuf[j, pad + tm - halo:pad + tm, :]

    h = h_s[...]
    tff = wa_ref.shape[1]
    for c0 in range(0, tff, FFN_COL_CHUNK):
        cols = slice(c0, min(c0 + FFN_COL_CHUNK, tff))
        a = _dot(h, wa_ref[:, cols])
        bb = _dot(h, wb_ref[:, cols])
        abuf[j, pad:pad + tm, cols] = a
        a1 = abuf[j, pad - shift:pad - shift + tm, cols]
        a2 = abuf[j, pad - 2 * shift:pad - 2 * shift + tm, cols]
        conv = cb_ref[:, cols] + a2 * cw_ref[0:1, cols] + a1 * cw_ref[1:2, cols] + a * cw_ref[2:3, cols]
        acc_s[...] += _dot((_gelu(conv) * bb).astype(BF16), wd_ref[cols, :])
        last_ref[0, :, cols] = a[tm - halo:, :]

    @pl.when(j == nff - 1)
    def _():
        xo_ref[...] = x_ref[...] + acc_s[...]


def _ffn(x, norm_w, w_up, w_down, conv_w, conv_b, conv_buf, tm, tff, shift, tiles_per_seq):
    t = x.shape[0]
    nff = D_FF // tff
    halo = (CONV_W - 1) * shift
    pad = -(-halo // SUBLANES) * SUBLANES
    n_tiles = t // tm
    wmode = dict(pipeline_mode=pl.Buffered(1)) if nff == 1 else {}
    return pl.pallas_call(
        functools.partial(_ffn_kernel, tm=tm, shift=shift, tiles_per_seq=tiles_per_seq, pad=pad),
        grid=(n_tiles, nff),
        in_specs=[
            pl.BlockSpec((tm, D_MODEL), lambda i, j: (i, 0)),
            pl.BlockSpec((1, D_MODEL), lambda i, j: (0, 0)),
            pl.BlockSpec((D_MODEL, tff), lambda i, j: (0, j), **wmode),
            pl.BlockSpec((D_MODEL, tff), lambda i, j: (0, nff + j), **wmode),
            pl.BlockSpec((tff, D_MODEL), lambda i, j: (j, 0), **wmode),
            pl.BlockSpec((CONV_W, tff), lambda i, j: (0, j)),
            pl.BlockSpec((1, tff), lambda i, j: (0, j)),
            pl.BlockSpec((1, halo, tff), lambda i, j: (i // tiles_per_seq, 0, j)),
        ],
        out_specs=[
            pl.BlockSpec((tm, D_MODEL), lambda i, j: (i, 0)),
            pl.BlockSpec((1, halo, tff), lambda i, j: (i, 0, j)),
        ],
        out_shape=[
            jax.ShapeDtypeStruct((t, D_MODEL), F32),
            jax.ShapeDtypeStruct((n_tiles, halo, D_FF), F32),
        ],
        scratch_shapes=[
            pltpu.VMEM((tm, D_MODEL), BF16),
            pltpu.VMEM((tm, D_MODEL), F32),
            pltpu.VMEM((nff, pad + tm, tff), F32),
        ],
        compiler_params=pltpu.CompilerParams(dimension_semantics=("arbitrary", "arbitrary")),
        name="conv_ffn",
    )(x, norm_w.reshape(1, D_MODEL), w_up, w_up, w_down, conv_w, conv_b.reshape(1, D_FF), conv_buf)


def _row_tile(t, want):
    return min(t, want)


def _layer(x, nseq, seq_len, layer, pos0, s0, conv_state, past, w, is_sample):
    t = nseq * seq_len
    tm = _row_tile(t, 512)
    act_dtype = F32 if is_sample else BF16
    z, zf = _inproj(x, w["norm_mix_w"], w["w_in"], _row_tile(t, 1024), 1536, act_dtype)

    chunk = min(A_CHUNK, seq_len)
    oa, s_new = _hgrn(z, zf, w["lb"], w["a_norm_w"], s0, nseq, seq_len, chunk,
                      math.gcd(seq_len // chunk, 8), act_dtype)

    tab_rows = max(seq_len, tm)
    tabs = _rope_tables(pos0, seq_len, tab_rows)
    qkv = _qkv_prep(z, tabs, w["q_norm_w"], w["k_norm_w"], tm, tab_rows // tm, not is_sample)
    lam_init = 0.8 - 0.6 * math.exp(-0.3 * layer)
    lam = (jnp.exp(jnp.sum(w["lambda_q1"] * w["lambda_k1"]))
           - jnp.exp(jnp.sum(w["lambda_q2"] * w["lambda_k2"])) + lam_init)
    if is_sample:
        q, k_rows, v_rows = qkv
        cache_k, cache_v, page_table = past
        ob = _sample_attn(q, k_rows, v_rows, cache_k, cache_v, page_table, layer, lam, w["b_subln_w"],
                          nseq, seq_len, math.gcd(page_table.shape[1], 16), 1.0 - lam_init)
    else:
        q, k_rows, v_rows, k_bf, v_bf = qkv
        ob = _prompt_attn(q, k_bf, v_bf, lam, w["b_subln_w"], nseq, seq_len, min(seq_len, 512),
                          min(seq_len, 512), 1.0 - lam_init, act_dtype)

    if is_sample:
        oc, vc = _token_mlp(z, w["c_ln_w"], w["c_ln_b"], w["c_w_s"], w["c_b_s"], t, t, seq_len, True,
                            act_dtype)
    else:
        (oc,) = _token_mlp(z, w["c_ln_w"], w["c_ln_b"], w["c_w_s"], w["c_b_s"], tm, C_CHUNK, C_CHUNK,
                           False, act_dtype)
        vc = None

    x_mid = _merge(oa, ob, oc, z, x, w["w_branch_a"], w["w_branch_b"], w["w_branch_c"], w["w_out"], tm)

    if is_sample:
        xt = x_mid.reshape(nseq, seq_len, D_MODEL).transpose(1, 0, 2).reshape(t, D_MODEL)
        cbuf = conv_state.transpose(1, 0, 2).reshape(1, (CONV_W - 1) * nseq, D_FF)
        xo, last = _ffn(xt, w["norm_ffn_w"], w["w_up"], w["w_down"], w["conv_w"], w["conv_b"], cbuf,
                        t, 1408, nseq, 1)
        x_new = xo.reshape(seq_len, nseq, D_MODEL).transpose(1, 0, 2).reshape(t, D_MODEL)
        new_conv = last.reshape(CONV_W - 1, nseq, D_FF).transpose(1, 0, 2)
    else:
        tiles_per_seq = seq_len // tm
        x_new, last = _ffn(x_mid, w["norm_ffn_w"], w["w_up"], w["w_down"], w["conv_w"], w["conv_b"],
                           conv_state, tm, D_FF, 1, tiles_per_seq)
        new_conv = last.reshape(nseq, tiles_per_seq, CONV_W - 1, D_FF)[:, -1]
    return x_new, s_new, k_rows, v_rows, vc, new_conv


def kernel(x_prompt, x_sample, cache_k, cache_v, page_table, state_hgrn, state_conv,
           norm_mix_w, w_in, lb_logits, a_norm_w, q_norm_w, k_norm_w,
           lambda_q1, lambda_k1, lambda_q2, lambda_k2, b_subln_w,
           c_ln_w, c_ln_b, c_w_s, c_b_s, w_branch_a, w_branch_b, w_branch_c, w_out,
           norm_ffn_w, w_up, conv_w, conv_b, w_down):
    depth = w_in.shape[0]
    n_p, len_p, _ = x_prompt.shape
    n_s, len_s, _ = x_sample.shape
    past_len = page_table.shape[1] * PAGE_SIZE
    p_lb = jax.nn.softmax(lb_logits.astype(F32), axis=0)
    lower_bounds = jnp.cumsum(p_lb, axis=0) - p_lb[0:1]

    xp = x_prompt.reshape(n_p * len_p, D_MODEL)
    xs = x_sample.reshape(n_s * len_s, D_MODEL)
    outs_p, outs_s = [], []
    for l in range(depth):
        w = dict(norm_mix_w=norm_mix_w[l], w_in=w_in[l].astype(BF16), lb=lower_bounds[l],
                 a_norm_w=a_norm_w[l], q_norm_w=q_norm_w[l], k_norm_w=k_norm_w[l],
                 lambda_q1=lambda_q1[l], lambda_k1=lambda_k1[l],
                 lambda_q2=lambda_q2[l], lambda_k2=lambda_k2[l], b_subln_w=b_subln_w[l],
                 c_ln_w=c_ln_w[l], c_ln_b=c_ln_b[l], c_w_s=c_w_s[l], c_b_s=c_b_s[l],
                 w_branch_a=w_branch_a[l].astype(BF16), w_branch_b=w_branch_b[l].astype(BF16),
                 w_branch_c=w_branch_c[l].astype(BF16), w_out=w_out[l].astype(BF16),
                 norm_ffn_w=norm_ffn_w[l], w_up=w_up[l].astype(BF16), conv_w=conv_w[l],
                 conv_b=conv_b[l], w_down=w_down[l].astype(BF16))
        s0_p = jnp.zeros((n_p, N_HEADS, HEAD_W, HEAD_W), F32)
        conv0_p = jnp.zeros((n_p, CONV_W - 1, D_FF), F32)
        xp, *rest_p = _layer(xp, n_p, len_p, l, 0, s0_p, conv0_p, None, w, False)
        outs_p.append(rest_p)
        xs, *rest_s = _layer(xs, n_s, len_s, l, past_len, state_hgrn[l], state_conv[l],
                             (cache_k, cache_v, page_table), w, True)
        outs_s.append(rest_s)

    def stack(outs, idx, shape):
        return jnp.stack([o[idx].reshape(shape) for o in outs])

    kv_p = (n_p, len_p, N_HEADS, HEAD_W)
    kv_s = (n_s, len_s, N_HEADS, HEAD_W)
    return (xp.reshape(n_p, len_p, D_MODEL), xs.reshape(n_s, len_s, D_MODEL),
            stack(outs_p, 1, kv_p), stack(outs_p, 2, kv_p),
            stack(outs_p, 0, (n_p, N_HEADS, HEAD_W, HEAD_W)),
            stack(outs_p, 4, (n_p, CONV_W - 1, D_FF)),
            stack(outs_s, 1, kv_s), stack(outs_s, 2, kv_s),
            stack(outs_s, 0, (n_s, N_HEADS, HEAD_W, HEAD_W)),
            stack(outs_s, 4, (n_s, CONV_W - 1, D_FF)),
            stack(outs_s, 3, (n_s, len_s, BRANCH_W)))
```

```python
import functools
import math

import jax
import jax.numpy as jnp
from jax import lax
from jax.experimental import pallas as pl
from jax.experimental.pallas import tpu as pltpu

F32 = jnp.float32
BF16 = jnp.bfloat16

D_MODEL = 1024
BRANCH_W = 512
N_HEADS = 4
HEAD_W = 128
QK_DH = 64
ROPE_THETA = 10000.0
A_CHUNK = 64
C_CHUNK = 128
PAGE_SIZE = 128
D_FF = 2816
CONV_W = 3
IN_COLS = 9 * BRANCH_W + 3 * D_MODEL
EPS = 1e-6
NEG_INF = -1e30
LANES = 128
SUBLANES = 8

COL_AQ, COL_AF, COL_AI, COL_AG, COL_BQ, COL_BK, COL_BV, COL_CU, COL_CV = range(9)
GATE_BLOCK_W = 1536
MXU_TILE = 256
FFN_COL_CHUNK = 6 * MXU_TILE


def _sigmoid(x):
    return 1.0 / (1.0 + jnp.exp(-x))


def _sigmoid_gate(x):
    return 0.5 * jnp.tanh(0.5 * x) + 0.5


def _gelu(x):
    c = math.sqrt(2.0 / math.pi)
    return 0.5 * x * (1.0 + jnp.tanh(c * (x + 0.044715 * (x * x * x))))


def _dot(a, b):
    return jnp.dot(a, b, preferred_element_type=F32)


def _dot_nt(a, b):
    return lax.dot_general(a, b, (((1,), (1,)), ((), ())), preferred_element_type=F32)


def _dot_tn(a, b):
    return lax.dot_general(a, b, (((0,), (0,)), ((), ())), preferred_element_type=F32)


def _cumsum_rows(x):
    row = lax.broadcasted_iota(jnp.int32, x.shape, 0)
    d = 1
    while d < x.shape[0]:
        x = x + jnp.where(row >= d, pltpu.roll(x, d, 0), 0.0)
        d *= 2
    return x


def _pad_rows(a, rows):
    if a.shape[0] >= rows:
        return a
    return jnp.concatenate([a, jnp.zeros((rows - a.shape[0],) + a.shape[1:], a.dtype)], axis=0)


def _log2(n):
    assert n & (n - 1) == 0, n
    return n.bit_length() - 1


def _inproj_kernel(x_ref, nw_ref, w_ref, z_ref, zf_ref, h_ref):
    j = pl.program_id(1)

    @pl.when(j == 0)
    def _():
        x = x_ref[...]
        ms = jnp.mean(x * x, axis=-1, keepdims=True)
        h_ref[...] = (x * lax.rsqrt(ms + EPS) * nw_ref[...]).astype(BF16)

    z = _dot(h_ref[...], w_ref[...])
    z_ref[...] = z.astype(z_ref.dtype)

    @pl.when(j == 0)
    def _():
        zf_ref[...] = z[:, COL_AF * BRANCH_W:(COL_AF + 1) * BRANCH_W]


def _inproj(x, norm_w, w_bf16, tm, tn, z_dtype):
    t = x.shape[0]
    assert tn >= (COL_AF + 1) * BRANCH_W
    return pl.pallas_call(
        _inproj_kernel,
        grid=(t // tm, IN_COLS // tn),
        in_specs=[
            pl.BlockSpec((tm, D_MODEL), lambda i, j: (i, 0)),
            pl.BlockSpec((1, D_MODEL), lambda i, j: (0, 0)),
            pl.BlockSpec((D_MODEL, tn), lambda i, j: (0, j)),
        ],
        out_specs=[pl.BlockSpec((tm, tn), lambda i, j: (i, j)),
                   pl.BlockSpec((tm, BRANCH_W), lambda i, j: (i, 0))],
        out_shape=[jax.ShapeDtypeStruct((t, IN_COLS), z_dtype),
                   jax.ShapeDtypeStruct((t, BRANCH_W), F32)],
        scratch_shapes=[pltpu.VMEM((tm, D_MODEL), BF16)],
        compiler_params=pltpu.CompilerParams(dimension_semantics=("parallel", "arbitrary")),
        name="inproj",
    )(x, norm_w.reshape(1, D_MODEL), w_bf16)


def _hgrn_kernel(zq_ref, zf_ref, zi_ref, zg_ref, lb_ref, nw_ref, s0_ref, oa_ref, sfin_ref,
                 st_ref, c_s, b_s, *, chunk, blk, n_sub, seqs):
    c = pl.program_id(1)
    nc = pl.num_programs(1)

    @pl.when(c == 0)
    def _():
        for g in range(seqs):
            for h in range(N_HEADS):
                st_ref[g * N_HEADS + h] = s0_ref[g, h].T

    n_blk = chunk // blk
    t_loc = lax.broadcasted_iota(jnp.int32, (blk, HEAD_W), 0)
    lane = lax.broadcasted_iota(jnp.int32, (blk, LANES), 1)
    ones = jnp.ones((HEAD_W, LANES), BF16)
    mm_rows = max(chunk, 2 * SUBLANES)

    def one_chunk(rows, g=0):
        lb = lb_ref[...]
        sig = _sigmoid(zf_ref[rows, :])
        k_all = (1.0 - lb) * (1.0 - sig)
        b_all = _cumsum_rows(jnp.log(lb + (1.0 - lb) * sig))
        c_all = jnp.log(k_all) - b_all
        for h in range(N_HEADS):
            sl = slice(h * HEAD_W, (h + 1) * HEAD_W)
            k = k_all[:, sl]
            b = b_all[:, sl]
            zq = zq_ref[rows, sl].astype(F32)
            q = zq * _sigmoid_gate(zq)
            v = zi_ref[rows, sl].astype(F32)
            gh = g * N_HEADS + h
            c_s[gh] = c_all[:, sl]
            b_s[gh] = b

            row_blocks = []
            for i in range(n_blk):
                r0 = i * blk
                q_i = q[r0:r0 + blk]
                b_i = b[r0:r0 + blk]
                terms, starts, n_rows = [], [], 0
                for s in range(blk):
                    g0 = (s // SUBLANES) * SUBLANES
                    cs = c_s[gh, r0 + s:r0 + s + 1, :]
                    e = jnp.exp(jnp.where(t_loc[g0:] >= s, b_i[g0:] + cs, NEG_INF))
                    terms.append(q_i[g0:] * e)
                    starts.append(n_rows - g0)
                    n_rows += blk - g0
                sums = _dot(jnp.concatenate(terms, axis=0).astype(BF16), ones)
                groups = []
                for t0 in range(0, blk, SUBLANES):
                    sc_g = jnp.zeros((SUBLANES, LANES), F32)
                    for s in range(min(blk, t0 + SUBLANES)):
                        rows_s = sums[starts[s] + t0:starts[s] + t0 + SUBLANES]
                        sc_g = jnp.where(lane[:SUBLANES] == r0 + s, rows_s, sc_g)
                    groups.append(sc_g)
                sc_i = jnp.concatenate(groups, axis=0)
                if i > 0:
                    b0 = b_s[gh, r0:r0 + 1, :]
                    a_i = _pad_rows(q_i * jnp.exp(b_i - b0), 2 * SUBLANES).astype(BF16)
                    k_left = _pad_rows(k[:r0] * jnp.exp(b0 - b[:r0]), LANES).astype(BF16)
                    sc_i = sc_i + _dot_nt(a_i, k_left)[:blk]
                row_blocks.append(sc_i)
            scores = jnp.concatenate(row_blocks, axis=0)

            st = st_ref[gh]
            v_pad = _pad_rows(v, LANES).astype(BF16)
            qd = _pad_rows(q * jnp.exp(b), mm_rows).astype(BF16)
            sc_bf = _pad_rows(scores, mm_rows).astype(BF16)
            o = (_dot_nt(qd, st.astype(BF16)) + _dot(sc_bf, v_pad))[:chunk]

            b_last = b[chunk - 1:chunk, :]
            kd = _pad_rows(k * jnp.exp(b_last - b), LANES).astype(BF16)
            st_ref[gh] = jnp.exp(b_last) * st + _dot_tn(v_pad, kd)

            ms = jnp.mean(o * o, axis=-1, keepdims=True)
            on = o * lax.rsqrt(ms + EPS) * nw_ref[...]
            oa_ref[rows, sl] = (on * _sigmoid_gate(zg_ref[rows, sl].astype(F32))).astype(oa_ref.dtype)

    if n_sub == 1:
        for g in range(seqs):
            one_chunk(pl.ds(g * chunk, chunk), g)
    else:
        def body(ci, carry):
            one_chunk(pl.ds(pl.multiple_of(ci * chunk, chunk), chunk))
            return carry

        lax.fori_loop(0, n_sub, body, 0, unroll=4)

    @pl.when(c == nc - 1)
    def _():
        for g in range(seqs):
            for h in range(N_HEADS):
                sfin_ref[g, h] = st_ref[g * N_HEADS + h].T


def _hgrn(z, zf, lb, norm_w, s0, nseq, seq_len, chunk, n_sub, seqs, out_dtype):
    assert seqs == 1 or (n_sub == 1 and chunk == seq_len)
    rows = chunk * n_sub * seqs
    nc = seq_len * seqs // rows
    zspec = lambda cb: pl.BlockSpec((rows, BRANCH_W), lambda b, c: (b * nc + c, cb))
    return pl.pallas_call(
        functools.partial(_hgrn_kernel, chunk=chunk, blk=min(chunk, 2 * SUBLANES), n_sub=n_sub, seqs=seqs),
        grid=(nseq // seqs, nc),
        in_specs=[
            zspec(COL_AQ), zspec(0), zspec(COL_AI), zspec(COL_AG),
            pl.BlockSpec((1, BRANCH_W), lambda b, c: (0, 0)),
            pl.BlockSpec((1, HEAD_W), lambda b, c: (0, 0)),
            pl.BlockSpec((seqs, N_HEADS, HEAD_W, HEAD_W), lambda b, c: (b, 0, 0, 0)),
        ],
        out_specs=[
            pl.BlockSpec((rows, BRANCH_W), lambda b, c: (b * nc + c, 0)),
            pl.BlockSpec((seqs, N_HEADS, HEAD_W, HEAD_W), lambda b, c: (b, 0, 0, 0)),
        ],
        out_shape=[
            jax.ShapeDtypeStruct((nseq * seq_len, BRANCH_W), out_dtype),
            jax.ShapeDtypeStruct((nseq, N_HEADS, HEAD_W, HEAD_W), F32),
        ],
        scratch_shapes=[
            pltpu.VMEM((seqs * N_HEADS, HEAD_W, HEAD_W), F32),
            pltpu.VMEM((seqs * N_HEADS, chunk, HEAD_W), F32),
            pltpu.VMEM((seqs * N_HEADS, chunk, HEAD_W), F32),
        ],
        compiler_params=pltpu.CompilerParams(dimension_semantics=("parallel", "arbitrary")),
        name="hgrn",
    )(z, zf, z, z, lb.reshape(1, BRANCH_W), norm_w.reshape(1, HEAD_W), s0)


def _qkv_kernel(zq_ref, zk_ref, zv_ref, cos_ref, sa_ref, sb_ref, qw_ref, kw_ref, g_ref,
                q_out, kf_out, vf_out, *bf_outs):
    cos = cos_ref[...]
    sa = sa_ref[...]
    sb = sb_ref[...]
    gmat = g_ref[...]

    def norm_rope(t, w):
        sq = t * t
        hi = sq.astype(BF16)
        lo = (sq - hi.astype(F32)).astype(BF16)
        ss = _dot(hi, gmat) + _dot(lo, gmat)
        y = t * lax.rsqrt(ss * (1.0 / QK_DH) + EPS) * w
        return (y * cos + pltpu.roll(y, LANES - QK_DH // 2, 1) * sa
                + pltpu.roll(y, QK_DH // 2, 1) * sb)

    for h in range(N_HEADS):
        sl = slice(h * HEAD_W, (h + 1) * HEAD_W)
        qr = norm_rope(zq_ref[:, sl].astype(F32), qw_ref[...])
        kr = norm_rope(zk_ref[:, sl].astype(F32), kw_ref[...])
        q_out[:, sl] = (qr * (QK_DH ** -0.5)).astype(q_out.dtype)
        rows = pl.ds(h, kr.shape[0], stride=N_HEADS)
        kf_out[rows, :] = kr
        v = zv_ref[:, sl].astype(F32)
        vf_out[rows, :] = v
        if bf_outs:
            bf_outs[0][:, sl] = kr.astype(BF16)
            bf_outs[1][:, sl] = v.astype(BF16)


def _qkv_prep(z, tabs, q_norm_w, k_norm_w, tm, tab_blocks, emit_bf16):
    t = z.shape[0]
    zspec = lambda cb: pl.BlockSpec((tm, BRANCH_W), lambda i: (i, cb))
    tspec = pl.BlockSpec((tm, LANES), lambda i: (i % tab_blocks, 0))
    wspec = pl.BlockSpec((1, LANES), lambda i: (0, 0))
    ospec = pl.BlockSpec((tm, BRANCH_W), lambda i: (i, 0))
    rspec = pl.BlockSpec((tm * N_HEADS, HEAD_W), lambda i: (i, 0))
    gi = lax.broadcasted_iota(jnp.int32, (LANES, LANES), 0) // QK_DH
    gj = lax.broadcasted_iota(jnp.int32, (LANES, LANES), 1) // QK_DH
    gmat = (gi == gj).astype(BF16)
    n_out = 5 if emit_bf16 else 3
    dts = [BF16 if emit_bf16 else F32, F32, F32, BF16, BF16][:n_out]
    return pl.pallas_call(
        _qkv_kernel,
        grid=(t // tm,),
        in_specs=[zspec(COL_BQ), zspec(COL_BK), zspec(COL_BV), tspec, tspec, tspec, wspec, wspec,
                  pl.BlockSpec((LANES, LANES), lambda i: (0, 0))],
        out_specs=[ospec, rspec, rspec] + [ospec] * (n_out - 3),
        out_shape=[jax.ShapeDtypeStruct((t * N_HEADS, HEAD_W) if i in (1, 2) else (t, BRANCH_W), d)
                   for i, d in enumerate(dts)],
        compiler_params=pltpu.CompilerParams(dimension_semantics=("parallel",)),
        name="qkv_prep",
    )(z, z, z, *tabs, jnp.tile(q_norm_w, 2).reshape(1, LANES),
      jnp.tile(k_norm_w, 2).reshape(1, LANES), gmat)


def _rope_tables(pos0, seq_len, rows):
    half = QK_DH // 2
    inv = ROPE_THETA ** (-jnp.arange(half, dtype=F32) / half)
    pos = (pos0 + jnp.arange(seq_len, dtype=jnp.int32)).astype(F32)
    ang = pos[:, None] * inv[None, :]
    cos, sin = jnp.cos(ang), jnp.sin(ang)
    zero = jnp.zeros_like(sin)
    per_head = lambda a, b: jnp.tile(jnp.concatenate([a, b], axis=-1), (rows // seq_len, LANES // QK_DH))
    return per_head(cos, cos), per_head(-sin, zero), per_head(zero, sin)


def _softmax_step(s, v_bf16, m_ref, l_ref, acc_ref):
    m_old = m_ref[...]
    m_new = jnp.maximum(m_old, jnp.max(s, axis=-1, keepdims=True))
    alpha = jnp.exp(m_old - m_new)
    p = jnp.exp((s - jnp.concatenate([m_new] * (s.shape[1] // LANES), axis=1)).astype(BF16))
    l_ref[...] = alpha * l_ref[...] + jnp.sum(p.astype(F32), axis=-1, keepdims=True)
    acc_ref[...] = alpha * acc_ref[...] + _dot(p, v_bf16)
    m_ref[...] = m_new


def _subln(o, w, scale):
    ms = jnp.mean(o * o, axis=-1, keepdims=True)
    return o * lax.rsqrt(ms + EPS) * w * scale


def _attn_kernel(lam_ref, w_ref, q_ref, k_ref, v_ref, o_ref, q1_s, q2_s, m1, l1, a1, m2, l2, a2,
                 *, tq, tk, out_scale):
    nq = q_ref.shape[0] // tq
    kpq = tq // tk

    def q_rows(i):
        return pl.ds(pl.multiple_of(i * tq, tq), tq)

    def update(qi, ki, masked):
        k_rows = pl.ds(pl.multiple_of(ki * tk, tk), tk)
        k = k_ref[k_rows, :]
        v = v_ref[k_rows, :]
        s1 = _dot_nt(q1_s[...], k)
        s2 = _dot_nt(q2_s[...], k)
        if masked:
            ahead = (lax.broadcasted_iota(jnp.int32, s1.shape, 1)
                     - lax.broadcasted_iota(jnp.int32, s1.shape, 0))
            visible = ahead <= qi * tq - ki * tk
            s1 = jnp.where(visible, s1, NEG_INF)
            s2 = jnp.where(visible, s2, NEG_INF)
        _softmax_step(s1, v, m1, l1, a1)
        _softmax_step(s2, v, m2, l2, a2)

    def query_block(qi, carry):
        q = q_ref[q_rows(qi), :]
        lane = lax.broadcasted_iota(jnp.int32, q.shape, 1)
        zero = jnp.zeros_like(q)
        q1_s[...] = jnp.where(lane < QK_DH, q, zero)
        q2_s[...] = jnp.where(lane >= QK_DH, q, zero)
        for m, l, a in ((m1, l1, a1), (m2, l2, a2)):
            m[...] = jnp.full(m.shape, NEG_INF, F32)
            l[...] = jnp.zeros(l.shape, F32)
            a[...] = jnp.zeros(a.shape, F32)

        def key_block_pair(j, c):
            update(qi, 2 * j, False)
            update(qi, 2 * j + 1, False)
            return c

        n_past = qi * kpq
        lax.fori_loop(0, n_past // 2, key_block_pair, 0)

        @pl.when(n_past % 2 == 1)
        def _():
            update(qi, n_past - 1, False)

        for d in range(kpq):
            update(qi, qi * kpq + d, True)
        o = a1[...] / l1[...] - lam_ref[...] * (a2[...] / l2[...])
        o_ref[q_rows(qi), :] = _subln(o, w_ref[...], out_scale).astype(o_ref.dtype)
        return carry

    lax.fori_loop(0, nq, query_block, 0)


def _prompt_attn(q, k, v, lam, subln_w, nseq, seq_len, tq, tk, out_scale, out_dtype):
    sspec = pl.BlockSpec((seq_len, HEAD_W), lambda b, h: (b, h))
    cspec = pl.BlockSpec((1, HEAD_W), lambda b, h: (0, 0))
    qbuf = pltpu.VMEM((tq, HEAD_W), q.dtype)
    stat = pltpu.VMEM((tq, LANES), F32)
    acc = pltpu.VMEM((tq, HEAD_W), F32)
    return pl.pallas_call(
        functools.partial(_attn_kernel, tq=tq, tk=tk, out_scale=out_scale),
        grid=(nseq, N_HEADS),
        in_specs=[cspec, cspec, sspec, sspec, sspec],
        out_specs=sspec,
        out_shape=jax.ShapeDtypeStruct((nseq * seq_len, BRANCH_W), out_dtype),
        scratch_shapes=[qbuf, qbuf, stat, stat, acc, stat, stat, acc],
        compiler_params=pltpu.CompilerParams(dimension_semantics=("parallel", "parallel")),
        name="prompt_attn",
    )(jnp.full((1, HEAD_W), lam, F32), subln_w.reshape(1, HEAD_W), q, k, v)


def _sattn_kernel(pt_ref, lam_ref, w_ref, q_ref, kn_ref, vn_ref, *rest, pages, dec_seq, out_scale):
    del pt_ref
    k_refs = rest[:pages]
    v_refs = rest[pages:2 * pages]
    o_ref = rest[2 * pages]
    qall, m_ref, l_ref, acc_ref = rest[2 * pages + 1:]
    g = pl.program_id(1)
    ng = pl.num_programs(1)
    hr = 2 * dec_seq

    @pl.when(g == 0)
    def _():
        lane = lax.broadcasted_iota(jnp.int32, (dec_seq, HEAD_W), 1)
        parts = []
        for h in range(N_HEADS):
            qh = q_ref[:, h * HEAD_W:(h + 1) * HEAD_W]
            parts += [jnp.where(lane < QK_DH, qh, 0.0), jnp.where(lane >= QK_DH, qh, 0.0)]
        qall[...] = jnp.concatenate(parts, axis=0).astype(BF16)
        m_ref[...] = jnp.full(m_ref.shape, NEG_INF, F32)
        l_ref[...] = jnp.zeros(l_ref.shape, F32)
        acc_ref[...] = jnp.zeros(acc_ref.shape, F32)

    def step(k_of_head, v_of_head, mask):
        s = jnp.concatenate([_dot_nt(qall[h * hr:(h + 1) * hr, :], k_of_head(h))
                             for h in range(N_HEADS)], axis=0)
        if mask is not None:
            s = jnp.where(mask(s.shape), s, NEG_INF)
        m_old = m_ref[...]
        m_new = jnp.maximum(m_old, jnp.max(s, axis=-1, keepdims=True))
        alpha = jnp.exp(m_old - m_new)
        p = jnp.exp(s - m_new)
        l_ref[...] = alpha * l_ref[...] + jnp.sum(p, axis=-1, keepdims=True)
        pv = jnp.concatenate([_dot(p[h * hr:(h + 1) * hr].astype(BF16), v_of_head(h))
                              for h in range(N_HEADS)], axis=0)
        acc_ref[...] = alpha * acc_ref[...] + pv
        m_ref[...] = m_new

    def paged(refs):
        return lambda h: jnp.concatenate(
            [r[pl.ds(h, PAGE_SIZE, stride=N_HEADS), :] for r in refs], axis=0).astype(BF16)

    step(paged(k_refs), paged(v_refs), None)

    @pl.when(g == ng - 1)
    def _():
        def fresh(ref):
            return lambda h: _pad_rows(ref[pl.ds(h, dec_seq, stride=N_HEADS), :], LANES).astype(BF16)

        def causal(shape):
            tok = lax.broadcasted_iota(jnp.int32, shape, 0) & (dec_seq - 1)
            return lax.broadcasted_iota(jnp.int32, shape, 1) <= tok

        step(fresh(kn_ref), fresh(vn_ref), causal)
        acc = acc_ref[...]
        l = l_ref[...]
        for h in range(N_HEADS):
            r1 = slice(h * hr, h * hr + dec_seq)
            r2 = slice(h * hr + dec_seq, (h + 1) * hr)
            o = acc[r1] / l[r1] - lam_ref[...] * (acc[r2] / l[r2])
            o_ref[:, h * HEAD_W:(h + 1) * HEAD_W] = _subln(o, w_ref[...], out_scale).astype(o_ref.dtype)


def _sample_attn(q, k_new, v_new, cache_k, cache_v, page_table, layer, lam, subln_w, nseq, dec_seq,
                 pages, out_scale):
    n_pages = page_table.shape[1]
    rows = 2 * N_HEADS * dec_seq
    depth, n_pool = cache_k.shape[:2]
    cache_k = cache_k.reshape(depth, n_pool, PAGE_SIZE * N_HEADS, HEAD_W)
    cache_v = cache_v.reshape(depth, n_pool, PAGE_SIZE * N_HEADS, HEAD_W)

    def page_spec(i):
        return pl.BlockSpec((None, None, PAGE_SIZE * N_HEADS, HEAD_W),
                            lambda b, g, pt: (layer, pt[b, g * pages + i], 0, 0))

    cspec = pl.BlockSpec((1, HEAD_W), lambda b, g, pt: (0, 0))
    rspec = pl.BlockSpec((dec_seq, BRANCH_W), lambda b, g, pt: (b, 0))
    nspec = pl.BlockSpec((dec_seq * N_HEADS, HEAD_W), lambda b, g, pt: (b, 0))
    grid_spec = pltpu.PrefetchScalarGridSpec(
        num_scalar_prefetch=1,
        grid=(nseq, n_pages // pages),
        in_specs=[cspec, cspec, rspec, nspec, nspec]
        + [page_spec(i) for i in range(pages)] * 2,
        out_specs=rspec,
        scratch_shapes=[
            pltpu.VMEM((rows, HEAD_W), BF16),
            pltpu.VMEM((rows, 1), F32),
            pltpu.VMEM((rows, 1), F32),
            pltpu.VMEM((rows, HEAD_W), F32),
        ],
    )
    return pl.pallas_call(
        functools.partial(_sattn_kernel, pages=pages, dec_seq=dec_seq, out_scale=out_scale),
        grid_spec=grid_spec,
        out_shape=jax.ShapeDtypeStruct((nseq * dec_seq, BRANCH_W), F32),
        compiler_params=pltpu.CompilerParams(dimension_semantics=("parallel", "arbitrary")),
        name="sample_attn",
    )(page_table, jnp.full((1, HEAD_W), lam, F32), subln_w.reshape(1, HEAD_W), q, k_new, v_new,
      *([cache_k] * pages), *([cache_v] * pages))


def _tmlp_kernel(zu_ref, zv_ref, lnw_ref, lnb_ref, wm_ref, bias_ref, oc_ref, *vc_out,
                 mix_rows, period, n_sub):
    u = _gelu(zu_ref[...].astype(F32))
    g = _gelu(zv_ref[...].astype(F32))
    xc = g - jnp.mean(g, axis=-1, keepdims=True)
    vc = xc * lax.rsqrt(jnp.mean(xc * xc, axis=-1, keepdims=True) + EPS) * lnw_ref[...] + lnb_ref[...]
    if vc_out:
        vc_out[0][...] = vc
    row = lax.broadcasted_iota(jnp.int32, (mix_rows, mix_rows), 0)
    col = lax.broadcasted_iota(jnp.int32, (mix_rows, mix_rows), 1)
    sh = _log2(period)
    allow = ((row >> sh) == (col >> sh)) & ((col & (period - 1)) <= (row & (period - 1)))
    for gi in range(N_HEADS):
        sl = slice(gi * HEAD_W, (gi + 1) * HEAD_W)
        wm = jnp.where(allow, wm_ref[gi], 0.0).astype(BF16)
        for sb in range(n_sub):
            rs = slice(sb * mix_rows, (sb + 1) * mix_rows)
            mixed = _dot(wm, vc[rs, sl].astype(BF16)) + bias_ref[:, sl]
            oc_ref[rs, sl] = (u[rs, sl] * mixed).astype(oc_ref.dtype)


def _token_mlp(z, ln_w, ln_b, w_s, b_s, tile_rows, mix_rows, period, emit_v, out_dtype):
    t = z.shape[0]
    reps = mix_rows // period
    wm = jnp.tile(w_s[:, :period, :period], (1, reps, reps))
    bias = jnp.tile(jnp.repeat(b_s[:, :period].T, HEAD_W, axis=1), (reps, 1))
    zspec = lambda cb: pl.BlockSpec((tile_rows, BRANCH_W), lambda i: (i, cb))
    vec = pl.BlockSpec((1, BRANCH_W), lambda i: (0, 0))
    ospec = pl.BlockSpec((tile_rows, BRANCH_W), lambda i: (i, 0))
    out_shape = [jax.ShapeDtypeStruct((t, BRANCH_W), out_dtype)]
    if emit_v:
        out_shape.append(jax.ShapeDtypeStruct((t, BRANCH_W), F32))
    return pl.pallas_call(
        functools.partial(_tmlp_kernel, mix_rows=mix_rows, period=period, n_sub=tile_rows // mix_rows),
        grid=(t // tile_rows,),
        in_specs=[zspec(COL_CU), zspec(COL_CV), vec, vec,
                  pl.BlockSpec((N_HEADS, mix_rows, mix_rows), lambda i: (0, 0, 0)),
                  pl.BlockSpec((mix_rows, BRANCH_W), lambda i: (0, 0))],
        out_specs=[ospec] * len(out_shape),
        out_shape=out_shape,
        compiler_params=pltpu.CompilerParams(dimension_semantics=("parallel",)),
        name="token_mlp",
    )(z, z, ln_w.reshape(1, BRANCH_W), ln_b.reshape(1, BRANCH_W), wm, bias)


def _merge_kernel(oa_ref, ob_ref, oc_ref, zg0_ref, zg1_ref, x_ref, wa_ref, wb_ref, wc_ref, wo_ref,
                  xo_ref):
    pa = _dot(oa_ref[...].astype(BF16), wa_ref[...])
    pb = _dot(ob_ref[...].astype(BF16), wb_ref[...])
    pc = _dot(oc_ref[...].astype(BF16), wc_ref[...])
    zg0 = zg0_ref[...].astype(F32)
    zg1 = zg1_ref[...].astype(F32)
    split = 2 * D_MODEL - GATE_BLOCK_W
    g_a = _sigmoid_gate(zg0[:, :D_MODEL])
    g_b = _sigmoid_gate(jnp.concatenate([zg0[:, D_MODEL:], zg1[:, :split]], axis=1))
    g_c = _sigmoid_gate(zg1[:, split:])
    merged = g_a * pa + g_b * pb + g_c * pc
    xo_ref[...] = x_ref[...] + _dot(merged.astype(BF16), wo_ref[...])


def _merge(oa, ob, oc, z, x, wa, wb, wc, wo, tm):
    t = x.shape[0]
    bspec = pl.BlockSpec((tm, BRANCH_W), lambda i: (i, 0))
    gspec = lambda cb: pl.BlockSpec((tm, GATE_BLOCK_W), lambda i: (i, cb))
    xspec = pl.BlockSpec((tm, D_MODEL), lambda i: (i, 0))
    wspec = pl.BlockSpec((BRANCH_W, D_MODEL), lambda i: (0, 0))
    return pl.pallas_call(
        _merge_kernel,
        grid=(t // tm,),
        in_specs=[bspec, bspec, bspec, gspec(3), gspec(4), xspec, wspec, wspec, wspec,
                  pl.BlockSpec((D_MODEL, D_MODEL), lambda i: (0, 0))],
        out_specs=xspec,
        out_shape=jax.ShapeDtypeStruct((t, D_MODEL), F32),
        compiler_params=pltpu.CompilerParams(dimension_semantics=("parallel",)),
        name="merge",
    )(oa, ob, oc, z, z, x, wa, wb, wc, wo)


def _ffn_kernel(x_ref, nw_ref, wa_ref, wb_ref, wd_ref, cw_ref, cb_ref, cbuf_ref, xo_ref, last_ref,
                h_s, acc_s, abuf, *, tm, shift, tiles_per_seq, pad):
    i = pl.program_id(0)
    j = pl.program_id(1)
    nff = pl.num_programs(1)
    halo = (CONV_W - 1) * shift

    @pl.when(j == 0)
    def _():
        x = x_ref[...]
        ms = jnp.mean(x * x, axis=-1, keepdims=True)
        h_s[...] = (x * lax.rsqrt(ms + EPS) * nw_ref[...]).astype(BF16)
        acc_s[...] = jnp.zeros(acc_s.shape, F32)

    @pl.when(i % tiles_per_seq == 0)
    def _():
        abuf[j, pad - halo:pad, :] = cbuf_ref[0]

    @pl.when(i % tiles_per_seq != 0)
    def _():
        abuf[j, pad - halo:pad, :] = abuf[j, pad + tm - halo:pad + tm, :]

    h = h_s[...]
    tff = wa_ref.shape[1]
    for c0 in range(0, tff, FFN_COL_CHUNK):
        cols = slice(c0, min(c0 + FFN_COL_CHUNK, tff))
        a = _dot(h, wa_ref[:, cols])
        bb = _dot(h, wb_ref[:, cols])
        abuf[j, pad:pad + tm, cols] = a
        a1 = abuf[j, pad - shift:pad - shift + tm, cols]
        a2 = abuf[j, pad - 2 * shift:pad - 2 * shift + tm, cols]
        conv = cb_ref[:, cols] + a2 * cw_ref[0:1, cols] + a1 * cw_ref[1:2, cols] + a * cw_ref[2:3, cols]
        acc_s[...] += _dot((_gelu(conv) * bb).astype(BF16), wd_ref[cols, :])
        last_ref[0, :, cols] = a[tm - halo:, :]

    @pl.when(j == nff - 1)
    def _():
        xo_ref[...] = x_ref[...] + acc_s[...]


def _ffn(x, norm_w, w_up, w_down, conv_w, conv_b, conv_buf, tm, tff, shift, tiles_per_seq):
    t = x.shape[0]
    nff = D_FF // tff
    halo = (CONV_W - 1) * shift
    pad = -(-halo // SUBLANES) * SUBLANES
    n_tiles = t // tm
    wmode = dict(pipeline_mode=pl.Buffered(1)) if nff == 1 else {}
    return pl.pallas_call(
        functools.partial(_ffn_kernel, tm=tm, shift=shift, tiles_per_seq=tiles_per_seq, pad=pad),
        grid=(n_tiles, nff),
        in_specs=[
            pl.BlockSpec((tm, D_MODEL), lambda i, j: (i, 0)),
            pl.BlockSpec((1, D_MODEL), lambda i, j: (0, 0)),
            pl.BlockSpec((D_MODEL, tff), lambda i, j: (0, j), **wmode),
            pl.BlockSpec((D_MODEL, tff), lambda i, j: (0, nff + j), **wmode),
            pl.BlockSpec((tff, D_MODEL), lambda i, j: (j, 0), **wmode),
            pl.BlockSpec((CONV_W, tff), lambda i, j: (0, j)),
            pl.BlockSpec((1, tff), lambda i, j: (0, j)),
            pl.BlockSpec((1, halo, tff), lambda i, j: (i // tiles_per_seq, 0, j)),
        ],
        out_specs=[
            pl.BlockSpec((tm, D_MODEL), lambda i, j: (i, 0)),
            pl.BlockSpec((1, halo, tff), lambda i, j: (i, 0, j)),
        ],
        out_shape=[
            jax.ShapeDtypeStruct((t, D_MODEL), F32),
            jax.ShapeDtypeStruct((n_tiles, halo, D_FF), F32),
        ],
        scratch_shapes=[
            pltpu.VMEM((tm, D_MODEL), BF16),
            pltpu.VMEM((tm, D_MODEL), F32),
            pltpu.VMEM((nff, pad + tm, tff), F32),
        ],
        compiler_params=pltpu.CompilerParams(dimension_semantics=("arbitrary", "arbitrary")),
        name="conv_ffn",
    )(x, norm_w.reshape(1, D_MODEL), w_up, w_up, w_down, conv_w, conv_b.reshape(1, D_FF), conv_buf)


def _row_tile(t, want):
    return min(t, want)


def _layer(x, nseq, seq_len, layer, pos0, s0, conv_state, past, w, is_sample):
    t = nseq * seq_len
    tm = _row_tile(t, 512)
    act_dtype = F32 if is_sample else BF16
    z, zf = _inproj(x, w["norm_mix_w"], w["w_in"], _row_tile(t, 1024), 1536, act_dtype)

    chunk = min(A_CHUNK, seq_len)
    n_sub = math.gcd(seq_len // chunk, 16)
    seqs = math.gcd(nseq, 4) if seq_len == chunk else 1
    oa, s_new = _hgrn(z, zf, w["lb"], w["a_norm_w"], s0, nseq, seq_len, chunk, n_sub, seqs, act_dtype)

    tab_rows = max(seq_len, tm)
    tabs = _rope_tables(pos0, seq_len, tab_rows)
    qkv = _qkv_prep(z, tabs, w["q_norm_w"], w["k_norm_w"], tm, tab_rows // tm, not is_sample)
    lam_init = 0.8 - 0.6 * math.exp(-0.3 * layer)
    lam = (jnp.exp(jnp.sum(w["lambda_q1"] * w["lambda_k1"]))
           - jnp.exp(jnp.sum(w["lambda_q2"] * w["lambda_k2"])) + lam_init)
    if is_sample:
        q, k_rows, v_rows = qkv
        cache_k, cache_v, page_table = past
        ob = _sample_attn(q, k_rows, v_rows, cache_k, cache_v, page_table, layer, lam, w["b_subln_w"],
                          nseq, seq_len, math.gcd(page_table.shape[1], 32), 1.0 - lam_init)
    else:
        q, k_rows, v_rows, k_bf, v_bf = qkv
        ob = _prompt_attn(q, k_bf, v_bf, lam, w["b_subln_w"], nseq, seq_len, min(seq_len, 512),
                          min(seq_len, 512), 1.0 - lam_init, act_dtype)

    if is_sample:
        oc, vc = _token_mlp(z, w["c_ln_w"], w["c_ln_b"], w["c_w_s"], w["c_b_s"], t, t, seq_len, True,
                            act_dtype)
    else:
        (oc,) = _token_mlp(z, w["c_ln_w"], w["c_ln_b"], w["c_w_s"], w["c_b_s"], tm, C_CHUNK, C_CHUNK,
                           False, act_dtype)
        vc = None

    x_mid = _merge(oa, ob, oc, z, x, w["w_branch_a"], w["w_branch_b"], w["w_branch_c"], w["w_out"], tm)

    if is_sample:
        xt = x_mid.reshape(nseq, seq_len, D_MODEL).transpose(1, 0, 2).reshape(t, D_MODEL)
        cbuf = conv_state.transpose(1, 0, 2).reshape(1, (CONV_W - 1) * nseq, D_FF)
        xo, last = _ffn(xt, w["norm_ffn_w"], w["w_up"], w["w_down"], w["conv_w"], w["conv_b"], cbuf,
                        t, 1408, nseq, 1)
        x_new = xo.reshape(seq_len, nseq, D_MODEL).transpose(1, 0, 2).reshape(t, D_MODEL)
        new_conv = last.reshape(CONV_W - 1, nseq, D_FF).transpose(1, 0, 2)
    else:
        tiles_per_seq = seq_len // tm
        x_new, last = _ffn(x_mid, w["norm_ffn_w"], w["w_up"], w["w_down"], w["conv_w"], w["conv_b"],
                           conv_state, tm, D_FF, 1, tiles_per_seq)
        new_conv = last.reshape(nseq, tiles_per_seq, CONV_W - 1, D_FF)[:, -1]
    return x_new, s_new, k_rows, v_rows, vc, new_conv


def kernel(x_prompt, x_sample, cache_k, cache_v, page_table, state_hgrn, state_conv,
           norm_mix_w, w_in, lb_logits, a_norm_w, q_norm_w, k_norm_w,
           lambda_q1, lambda_k1, lambda_q2, lambda_k2, b_subln_w,
           c_ln_w, c_ln_b, c_w_s, c_b_s, w_branch_a, w_branch_b, w_branch_c, w_out,
           norm_ffn_w, w_up, conv_w, conv_b, w_down):
    depth = w_in.shape[0]
    n_p, len_p, _ = x_prompt.shape
    n_s, len_s, _ = x_sample.shape
    past_len = page_table.shape[1] * PAGE_SIZE
    p_lb = jax.nn.softmax(lb_logits.astype(F32), axis=0)
    lower_bounds = jnp.cumsum(p_lb, axis=0) - p_lb[0:1]

    xp = x_prompt.reshape(n_p * len_p, D_MODEL)
    xs = x_sample.reshape(n_s * len_s, D_MODEL)
    outs_p, outs_s = [], []
    for l in range(depth):
        w = dict(norm_mix_w=norm_mix_w[l], w_in=w_in[l].astype(BF16), lb=lower_bounds[l],
                 a_norm_w=a_norm_w[l], q_norm_w=q_norm_w[l], k_norm_w=k_norm_w[l],
                 lambda_q1=lambda_q1[l], lambda_k1=lambda_k1[l],
                 lambda_q2=lambda_q2[l], lambda_k2=lambda_k2[l], b_subln_w=b_subln_w[l],
                 c_ln_w=c_ln_w[l], c_ln_b=c_ln_b[l], c_w_s=c_w_s[l], c_b_s=c_b_s[l],
                 w_branch_a=w_branch_a[l].astype(BF16), w_branch_b=w_branch_b[l].astype(BF16),
                 w_branch_c=w_branch_c[l].astype(BF16), w_out=w_out[l].astype(BF16),
                 norm_ffn_w=norm_ffn_w[l], w_up=w_up[l].astype(BF16), conv_w=conv_w[l],
                 conv_b=conv_b[l], w_down=w_down[l].astype(BF16))
        s0_p = jnp.zeros((n_p, N_HEADS, HEAD_W, HEAD_W), F32)
        conv0_p = jnp.zeros((n_p, CONV_W - 1, D_FF), F32)
        xp, *rest_p = _layer(xp, n_p, len_p, l, 0, s0_p, conv0_p, None, w, False)
        outs_p.append(rest_p)
        xs, *rest_s = _layer(xs, n_s, len_s, l, past_len, state_hgrn[l], state_conv[l],
                             (cache_k, cache_v, page_table), w, True)
        outs_s.append(rest_s)

    def stack(outs, idx, shape):
        return jnp.stack([o[idx].reshape(shape) for o in outs])

    kv_p = (n_p, len_p, N_HEADS, HEAD_W)
    kv_s = (n_s, len_s, N_HEADS, HEAD_W)
    return (xp.reshape(n_p, len_p, D_MODEL), xs.reshape(n_s, len_s, D_MODEL),
            stack(outs_p, 1, kv_p), stack(outs_p, 2, kv_p),
            stack(outs_p, 0, (n_p, N_HEADS, HEAD_W, HEAD_W)),
            stack(outs_p, 4, (n_p, CONV_W - 1, D_FF)),
            stack(outs_s, 1, kv_s), stack(outs_s, 2, kv_s),
            stack(outs_s, 0, (n_s, N_HEADS, HEAD_W, HEAD_W)),
            stack(outs_s, 4, (n_s, CONV_W - 1, D_FF)),
            stack(outs_s, 3, (n_s, len_s, BRANCH_W)))
```

```python
import functools
import math
from typing import NamedTuple

import jax
import jax.numpy as jnp
from jax import lax
from jax.experimental import pallas as pl
from jax.experimental.pallas import tpu as pltpu

F32 = jnp.float32
BF16 = jnp.bfloat16

D_MODEL = 1024
BRANCH_W = 512
N_HEADS = 4
HEAD_W = 128
QK_DH = 64
ROPE_THETA = 10000.0
A_CHUNK = 64
C_CHUNK = 128
PAGE_SIZE = 128
D_FF = 2816
CONV_W = 3
IN_COLS = 9 * BRANCH_W + 3 * D_MODEL
EPS = 1e-6
NEG_INF = -1e30
LANES = 128
SUBLANES = 8

COL_AQ, COL_AF, COL_AI, COL_AG, COL_BQ, COL_BK, COL_BV, COL_CU, COL_CV = range(9)
GATE_BLOCK_W = 1536
MXU_TILE = 256
FFN_COL_CHUNK = 6 * MXU_TILE


def _sigmoid(x):
    return 1.0 / (1.0 + jnp.exp(-x))


def _sigmoid_gate(x):
    return 0.5 * jnp.tanh(0.5 * x) + 0.5


def _gelu(x):
    c = math.sqrt(2.0 / math.pi)
    return 0.5 * x * (1.0 + jnp.tanh(c * (x + 0.044715 * (x * x * x))))


def _dot(a, b):
    return jnp.dot(a, b, preferred_element_type=F32)


def _dot_nt(a, b):
    return lax.dot_general(a, b, (((1,), (1,)), ((), ())), preferred_element_type=F32)


def _dot_tn(a, b):
    return lax.dot_general(a, b, (((0,), (0,)), ((), ())), preferred_element_type=F32)


def _cumsum_rows(x):
    row = lax.broadcasted_iota(jnp.int32, x.shape, 0)
    d = 1
    while d < x.shape[0]:
        x = x + jnp.where(row >= d, pltpu.roll(x, d, 0), 0.0)
        d *= 2
    return x


def _pad_rows(a, rows):
    if a.shape[0] >= rows:
        return a
    return jnp.concatenate([a, jnp.zeros((rows - a.shape[0],) + a.shape[1:], a.dtype)], axis=0)


def _log2(n):
    assert n & (n - 1) == 0, n
    return n.bit_length() - 1


def _inproj_kernel(x_ref, nw_ref, w_ref, z_ref, zf_ref, h_ref):
    j = pl.program_id(1)

    @pl.when(j == 0)
    def _():
        x = x_ref[...]
        ms = jnp.mean(x * x, axis=-1, keepdims=True)
        h_ref[...] = (x * lax.rsqrt(ms + EPS) * nw_ref[...]).astype(BF16)

    z = _dot(h_ref[...], w_ref[...])
    z_ref[...] = z.astype(z_ref.dtype)

    @pl.when(j == 0)
    def _():
        zf_ref[...] = z[:, COL_AF * BRANCH_W:(COL_AF + 1) * BRANCH_W]


def _inproj(x, norm_w, w_bf16, tm, tn, z_dtype):
    t = x.shape[0]
    assert tn >= (COL_AF + 1) * BRANCH_W
    return pl.pallas_call(
        _inproj_kernel,
        grid=(t // tm, IN_COLS // tn),
        in_specs=[
            pl.BlockSpec((tm, D_MODEL), lambda i, j: (i, 0)),
            pl.BlockSpec((1, D_MODEL), lambda i, j: (0, 0)),
            pl.BlockSpec((D_MODEL, tn), lambda i, j: (0, j)),
        ],
        out_specs=[pl.BlockSpec((tm, tn), lambda i, j: (i, j)),
                   pl.BlockSpec((tm, BRANCH_W), lambda i, j: (i, 0))],
        out_shape=[jax.ShapeDtypeStruct((t, IN_COLS), z_dtype),
                   jax.ShapeDtypeStruct((t, BRANCH_W), F32)],
        scratch_shapes=[pltpu.VMEM((tm, D_MODEL), BF16)],
        compiler_params=pltpu.CompilerParams(dimension_semantics=("parallel", "arbitrary")),
        name="inproj",
    )(x, norm_w.reshape(1, D_MODEL), w_bf16)


def _hgrn_kernel(zq_ref, zf_ref, zi_ref, zg_ref, lb_ref, nw_ref, s0_ref, oa_ref, sfin_ref,
                 st_ref, c_s, b_s, *, chunk, blk, n_sub, seqs):
    c = pl.program_id(1)
    nc = pl.num_programs(1)

    @pl.when(c == 0)
    def _():
        for g in range(seqs):
            for h in range(N_HEADS):
                st_ref[g * N_HEADS + h] = s0_ref[g, h].T

    n_blk = chunk // blk
    t_loc = lax.broadcasted_iota(jnp.int32, (blk, HEAD_W), 0)
    lane = lax.broadcasted_iota(jnp.int32, (blk, LANES), 1)
    ones = jnp.ones((HEAD_W, LANES), BF16)
    mm_rows = max(chunk, 2 * SUBLANES)

    def one_chunk(rows, g=0):
        lb = lb_ref[...]
        sig = _sigmoid(zf_ref[rows, :])
        k_all = (1.0 - lb) * (1.0 - sig)
        b_all = _cumsum_rows(jnp.log(lb + (1.0 - lb) * sig))
        c_all = jnp.log(k_all) - b_all
        for h in range(N_HEADS):
            sl = slice(h * HEAD_W, (h + 1) * HEAD_W)
            k = k_all[:, sl]
            b = b_all[:, sl]
            zq = zq_ref[rows, sl].astype(F32)
            q = zq * _sigmoid_gate(zq)
            v = zi_ref[rows, sl].astype(F32)
            gh = g * N_HEADS + h
            c_s[gh] = c_all[:, sl]
            b_s[gh] = b

            row_blocks = []
            for i in range(n_blk):
                r0 = i * blk
                q_i = q[r0:r0 + blk]
                b_i = b[r0:r0 + blk]
                terms, starts, n_rows = [], [], 0
                for s in range(blk):
                    g0 = (s // SUBLANES) * SUBLANES
                    cs = c_s[gh, r0 + s:r0 + s + 1, :]
                    e = jnp.exp(jnp.where(t_loc[g0:] >= s, b_i[g0:] + cs, NEG_INF))
                    terms.append(q_i[g0:] * e)
                    starts.append(n_rows - g0)
                    n_rows += blk - g0
                sums = _dot(jnp.concatenate(terms, axis=0).astype(BF16), ones)
                groups = []
                for t0 in range(0, blk, SUBLANES):
                    sc_g = jnp.zeros((SUBLANES, LANES), F32)
                    for s in range(min(blk, t0 + SUBLANES)):
                        rows_s = sums[starts[s] + t0:starts[s] + t0 + SUBLANES]
                        sc_g = jnp.where(lane[:SUBLANES] == r0 + s, rows_s, sc_g)
                    groups.append(sc_g)
                sc_i = jnp.concatenate(groups, axis=0)
                if i > 0:
                    b0 = b_s[gh, r0:r0 + 1, :]
                    a_i = _pad_rows(q_i * jnp.exp(b_i - b0), 2 * SUBLANES).astype(BF16)
                    k_left = _pad_rows(k[:r0] * jnp.exp(b0 - b[:r0]), LANES).astype(BF16)
                    sc_i = sc_i + _dot_nt(a_i, k_left)[:blk]
                row_blocks.append(sc_i)
            scores = jnp.concatenate(row_blocks, axis=0)

            st = st_ref[gh]
            v_pad = _pad_rows(v, LANES).astype(BF16)
            qd = _pad_rows(q * jnp.exp(b), mm_rows).astype(BF16)
            sc_bf = _pad_rows(scores, mm_rows).astype(BF16)
            o = (_dot_nt(qd, st.astype(BF16)) + _dot(sc_bf, v_pad))[:chunk]

            b_last = b[chunk - 1:chunk, :]
            kd = _pad_rows(k * jnp.exp(b_last - b), LANES).astype(BF16)
            st_ref[gh] = jnp.exp(b_last) * st + _dot_tn(v_pad, kd)

            ms = jnp.mean(o * o, axis=-1, keepdims=True)
            on = o * lax.rsqrt(ms + EPS) * nw_ref[...]
            oa_ref[rows, sl] = (on * _sigmoid_gate(zg_ref[rows, sl].astype(F32))).astype(oa_ref.dtype)

    if n_sub == 1:
        for g in range(seqs):
            one_chunk(pl.ds(g * chunk, chunk), g)
    else:
        def body(ci, carry):
            one_chunk(pl.ds(pl.multiple_of(ci * chunk, chunk), chunk))
            return carry

        lax.fori_loop(0, n_sub, body, 0, unroll=4)

    @pl.when(c == nc - 1)
    def _():
        for g in range(seqs):
            for h in range(N_HEADS):
                sfin_ref[g, h] = st_ref[g * N_HEADS + h].T


def _hgrn(z, zf, lb, norm_w, s0, nseq, seq_len, chunk, n_sub, seqs, out_dtype):
    assert seqs == 1 or (n_sub == 1 and chunk == seq_len)
    rows = chunk * n_sub * seqs
    nc = seq_len * seqs // rows
    zspec = lambda cb: pl.BlockSpec((rows, BRANCH_W), lambda b, c: (b * nc + c, cb))
    return pl.pallas_call(
        functools.partial(_hgrn_kernel, chunk=chunk, blk=min(chunk, 2 * SUBLANES), n_sub=n_sub, seqs=seqs),
        grid=(nseq // seqs, nc),
        in_specs=[
            zspec(COL_AQ), zspec(0), zspec(COL_AI), zspec(COL_AG),
            pl.BlockSpec((1, BRANCH_W), lambda b, c: (0, 0)),
            pl.BlockSpec((1, HEAD_W), lambda b, c: (0, 0)),
            pl.BlockSpec((seqs, N_HEADS, HEAD_W, HEAD_W), lambda b, c: (b, 0, 0, 0)),
        ],
        out_specs=[
            pl.BlockSpec((rows, BRANCH_W), lambda b, c: (b * nc + c, 0)),
            pl.BlockSpec((seqs, N_HEADS, HEAD_W, HEAD_W), lambda b, c: (b, 0, 0, 0)),
        ],
        out_shape=[
            jax.ShapeDtypeStruct((nseq * seq_len, BRANCH_W), out_dtype),
            jax.ShapeDtypeStruct((nseq, N_HEADS, HEAD_W, HEAD_W), F32),
        ],
        scratch_shapes=[
            pltpu.VMEM((seqs * N_HEADS, HEAD_W, HEAD_W), F32),
            pltpu.VMEM((seqs * N_HEADS, chunk, HEAD_W), F32),
            pltpu.VMEM((seqs * N_HEADS, chunk, HEAD_W), F32),
        ],
        compiler_params=pltpu.CompilerParams(dimension_semantics=("parallel", "arbitrary")),
        name="hgrn",
    )(z, zf, z, z, lb.reshape(1, BRANCH_W), norm_w.reshape(1, HEAD_W), s0)


def _qkv_kernel(zq_ref, zk_ref, zv_ref, cos_ref, sa_ref, sb_ref, qw_ref, kw_ref, g_ref,
                q_out, kf_out, vf_out, *bf_outs):
    cos = cos_ref[...]
    sa = sa_ref[...]
    sb = sb_ref[...]
    gmat = g_ref[...]

    def norm_rope(t, w):
        sq = t * t
        hi = sq.astype(BF16)
        lo = (sq - hi.astype(F32)).astype(BF16)
        ss = _dot(hi, gmat) + _dot(lo, gmat)
        y = t * lax.rsqrt(ss * (1.0 / QK_DH) + EPS) * w
        return (y * cos + pltpu.roll(y, LANES - QK_DH // 2, 1) * sa
                + pltpu.roll(y, QK_DH // 2, 1) * sb)

    for h in range(N_HEADS):
        sl = slice(h * HEAD_W, (h + 1) * HEAD_W)
        qr = norm_rope(zq_ref[:, sl].astype(F32), qw_ref[...])
        kr = norm_rope(zk_ref[:, sl].astype(F32), kw_ref[...])
        q_out[:, sl] = (qr * (QK_DH ** -0.5)).astype(q_out.dtype)
        rows = pl.ds(h, kr.shape[0], stride=N_HEADS)
        kf_out[rows, :] = kr
        v = zv_ref[:, sl].astype(F32)
        vf_out[rows, :] = v
        if bf_outs:
            bf_outs[0][:, sl] = kr.astype(BF16)
            bf_outs[1][:, sl] = v.astype(BF16)


def _qkv_prep(z, tabs, q_norm_w, k_norm_w, tm, tab_blocks, emit_bf16):
    t = z.shape[0]
    zspec = lambda cb: pl.BlockSpec((tm, BRANCH_W), lambda i: (i, cb))
    tspec = pl.BlockSpec((tm, LANES), lambda i: (i % tab_blocks, 0))
    wspec = pl.BlockSpec((1, LANES), lambda i: (0, 0))
    ospec = pl.BlockSpec((tm, BRANCH_W), lambda i: (i, 0))
    rspec = pl.BlockSpec((tm * N_HEADS, HEAD_W), lambda i: (i, 0))
    gi = lax.broadcasted_iota(jnp.int32, (LANES, LANES), 0) // QK_DH
    gj = lax.broadcasted_iota(jnp.int32, (LANES, LANES), 1) // QK_DH
    gmat = (gi == gj).astype(BF16)
    n_out = 5 if emit_bf16 else 3
    dts = [BF16 if emit_bf16 else F32, F32, F32, BF16, BF16][:n_out]
    return pl.pallas_call(
        _qkv_kernel,
        grid=(t // tm,),
        in_specs=[zspec(COL_BQ), zspec(COL_BK), zspec(COL_BV), tspec, tspec, tspec, wspec, wspec,
                  pl.BlockSpec((LANES, LANES), lambda i: (0, 0))],
        out_specs=[ospec, rspec, rspec] + [ospec] * (n_out - 3),
        out_shape=[jax.ShapeDtypeStruct((t * N_HEADS, HEAD_W) if i in (1, 2) else (t, BRANCH_W), d)
                   for i, d in enumerate(dts)],
        compiler_params=pltpu.CompilerParams(dimension_semantics=("parallel",)),
        name="qkv_prep",
    )(z, z, z, *tabs, jnp.tile(q_norm_w, 2).reshape(1, LANES),
      jnp.tile(k_norm_w, 2).reshape(1, LANES), gmat)


def _rope_tables(pos0, seq_len, rows):
    half = QK_DH // 2
    inv = ROPE_THETA ** (-jnp.arange(half, dtype=F32) / half)
    pos = (pos0 + jnp.arange(seq_len, dtype=jnp.int32)).astype(F32)
    ang = pos[:, None] * inv[None, :]
    cos, sin = jnp.cos(ang), jnp.sin(ang)
    zero = jnp.zeros_like(sin)
    per_head = lambda a, b: jnp.tile(jnp.concatenate([a, b], axis=-1), (rows // seq_len, LANES // QK_DH))
    return per_head(cos, cos), per_head(-sin, zero), per_head(zero, sin)


def _softmax_step(s, v_bf16, m_ref, l_ref, acc_ref):
    m_old = m_ref[...]
    m_new = jnp.maximum(m_old, jnp.max(s, axis=-1, keepdims=True))
    alpha = jnp.exp(m_old - m_new)
    p = jnp.exp((s - jnp.concatenate([m_new] * (s.shape[1] // LANES), axis=1)).astype(BF16))
    l_ref[...] = alpha * l_ref[...] + jnp.sum(p.astype(F32), axis=-1, keepdims=True)
    acc_ref[...] = alpha * acc_ref[...] + _dot(p, v_bf16)
    m_ref[...] = m_new


def _subln(o, w, scale):
    ms = jnp.mean(o * o, axis=-1, keepdims=True)
    return o * lax.rsqrt(ms + EPS) * w * scale


def _attn_kernel(lam_ref, w_ref, q_ref, k_ref, v_ref, o_ref, q1_s, q2_s, m1, l1, a1, m2, l2, a2,
                 *, tq, tk, out_scale):
    nq = q_ref.shape[0] // tq
    kpq = tq // tk

    def q_rows(i):
        return pl.ds(pl.multiple_of(i * tq, tq), tq)

    def update(qi, ki, masked):
        k_rows = pl.ds(pl.multiple_of(ki * tk, tk), tk)
        k = k_ref[k_rows, :]
        v = v_ref[k_rows, :]
        s1 = _dot_nt(q1_s[...], k)
        s2 = _dot_nt(q2_s[...], k)
        if masked:
            ahead = (lax.broadcasted_iota(jnp.int32, s1.shape, 1)
                     - lax.broadcasted_iota(jnp.int32, s1.shape, 0))
            visible = ahead <= qi * tq - ki * tk
            s1 = jnp.where(visible, s1, NEG_INF)
            s2 = jnp.where(visible, s2, NEG_INF)
        _softmax_step(s1, v, m1, l1, a1)
        _softmax_step(s2, v, m2, l2, a2)

    def query_block(qi, carry):
        q = q_ref[q_rows(qi), :]
        lane = lax.broadcasted_iota(jnp.int32, q.shape, 1)
        zero = jnp.zeros_like(q)
        q1_s[...] = jnp.where(lane < QK_DH, q, zero)
        q2_s[...] = jnp.where(lane >= QK_DH, q, zero)
        for m, l, a in ((m1, l1, a1), (m2, l2, a2)):
            m[...] = jnp.full(m.shape, NEG_INF, F32)
            l[...] = jnp.zeros(l.shape, F32)
            a[...] = jnp.zeros(a.shape, F32)

        def key_block_pair(j, c):
            update(qi, 2 * j, False)
            update(qi, 2 * j + 1, False)
            return c

        n_past = qi * kpq
        lax.fori_loop(0, n_past // 2, key_block_pair, 0)

        @pl.when(n_past % 2 == 1)
        def _():
            update(qi, n_past - 1, False)

        for d in range(kpq):
            update(qi, qi * kpq + d, True)
        o = a1[...] / l1[...] - lam_ref[...] * (a2[...] / l2[...])
        o_ref[q_rows(qi), :] = _subln(o, w_ref[...], out_scale).astype(o_ref.dtype)
        return carry

    lax.fori_loop(0, nq, query_block, 0)


def _prompt_attn(q, k, v, lam, subln_w, nseq, seq_len, tq, tk, out_scale, out_dtype):
    sspec = pl.BlockSpec((seq_len, HEAD_W), lambda b, h: (b, h))
    cspec = pl.BlockSpec((1, HEAD_W), lambda b, h: (0, 0))
    qbuf = pltpu.VMEM((tq, HEAD_W), q.dtype)
    stat = pltpu.VMEM((tq, LANES), F32)
    acc = pltpu.VMEM((tq, HEAD_W), F32)
    return pl.pallas_call(
        functools.partial(_attn_kernel, tq=tq, tk=tk, out_scale=out_scale),
        grid=(nseq, N_HEADS),
        in_specs=[cspec, cspec, sspec, sspec, sspec],
        out_specs=sspec,
        out_shape=jax.ShapeDtypeStruct((nseq * seq_len, BRANCH_W), out_dtype),
        scratch_shapes=[qbuf, qbuf, stat, stat, acc, stat, stat, acc],
        compiler_params=pltpu.CompilerParams(dimension_semantics=("parallel", "parallel")),
        name="prompt_attn",
    )(jnp.full((1, HEAD_W), lam, F32), subln_w.reshape(1, HEAD_W), q, k, v)


def _sattn_kernel(pt_ref, lam_ref, w_ref, q_ref, kn_ref, vn_ref, *rest, pages, dec_seq, out_scale):
    del pt_ref
    k_refs = rest[:pages]
    v_refs = rest[pages:2 * pages]
    o_ref = rest[2 * pages]
    qall, m_ref, l_ref, acc_ref = rest[2 * pages + 1:]
    g = pl.program_id(1)
    ng = pl.num_programs(1)
    hr = 2 * dec_seq

    @pl.when(g == 0)
    def _():
        lane = lax.broadcasted_iota(jnp.int32, (dec_seq, HEAD_W), 1)
        parts = []
        for h in range(N_HEADS):
            qh = q_ref[:, h * HEAD_W:(h + 1) * HEAD_W]
            parts += [jnp.where(lane < QK_DH, qh, 0.0), jnp.where(lane >= QK_DH, qh, 0.0)]
        qall[...] = jnp.concatenate(parts, axis=0).astype(BF16)
        m_ref[...] = jnp.full(m_ref.shape, NEG_INF, F32)
        l_ref[...] = jnp.zeros(l_ref.shape, F32)
        acc_ref[...] = jnp.zeros(acc_ref.shape, F32)

    def step(k_of_head, v_of_head, mask):
        s = jnp.concatenate([_dot_nt(qall[h * hr:(h + 1) * hr, :], k_of_head(h))
                             for h in range(N_HEADS)], axis=0)
        if mask is not None:
            s = jnp.where(mask(s.shape), s, NEG_INF)
        m_old = m_ref[...]
        m_new = jnp.maximum(m_old, jnp.max(s, axis=-1, keepdims=True))
        alpha = jnp.exp(m_old - m_new)
        p = jnp.exp(s - m_new)
        l_ref[...] = alpha * l_ref[...] + jnp.sum(p, axis=-1, keepdims=True)
        pv = jnp.concatenate([_dot(p[h * hr:(h + 1) * hr].astype(BF16), v_of_head(h))
                              for h in range(N_HEADS)], axis=0)
        acc_ref[...] = alpha * acc_ref[...] + pv
        m_ref[...] = m_new

    def paged(refs):
        return lambda h: jnp.concatenate(
            [r[pl.ds(h, PAGE_SIZE, stride=N_HEADS), :] for r in refs], axis=0).astype(BF16)

    step(paged(k_refs), paged(v_refs), None)

    @pl.when(g == ng - 1)
    def _():
        def fresh(ref):
            return lambda h: _pad_rows(ref[pl.ds(h, dec_seq, stride=N_HEADS), :], LANES).astype(BF16)

        def causal(shape):
            tok = lax.broadcasted_iota(jnp.int32, shape, 0) & (dec_seq - 1)
            return lax.broadcasted_iota(jnp.int32, shape, 1) <= tok

        step(fresh(kn_ref), fresh(vn_ref), causal)
        acc = acc_ref[...]
        l = l_ref[...]
        for h in range(N_HEADS):
            r1 = slice(h * hr, h * hr + dec_seq)
            r2 = slice(h * hr + dec_seq, (h + 1) * hr)
            o = acc[r1] / l[r1] - lam_ref[...] * (acc[r2] / l[r2])
            o_ref[:, h * HEAD_W:(h + 1) * HEAD_W] = _subln(o, w_ref[...], out_scale).astype(o_ref.dtype)


def _sample_attn(q, k_new, v_new, cache_k, cache_v, page_table, layer, lam, subln_w, nseq, dec_seq,
                 pages, out_scale):
    n_pages = page_table.shape[1]
    rows = 2 * N_HEADS * dec_seq
    depth, n_pool = cache_k.shape[:2]
    cache_k = cache_k.reshape(depth, n_pool, PAGE_SIZE * N_HEADS, HEAD_W)
    cache_v = cache_v.reshape(depth, n_pool, PAGE_SIZE * N_HEADS, HEAD_W)

    def page_spec(i):
        return pl.BlockSpec((None, None, PAGE_SIZE * N_HEADS, HEAD_W),
                            lambda b, g, pt: (layer, pt[b, g * pages + i], 0, 0))

    cspec = pl.BlockSpec((1, HEAD_W), lambda b, g, pt: (0, 0))
    rspec = pl.BlockSpec((dec_seq, BRANCH_W), lambda b, g, pt: (b, 0))
    nspec = pl.BlockSpec((dec_seq * N_HEADS, HEAD_W), lambda b, g, pt: (b, 0))
    grid_spec = pltpu.PrefetchScalarGridSpec(
        num_scalar_prefetch=1,
        grid=(nseq, n_pages // pages),
        in_specs=[cspec, cspec, rspec, nspec, nspec]
        + [page_spec(i) for i in range(pages)] * 2,
        out_specs=rspec,
        scratch_shapes=[
            pltpu.VMEM((rows, HEAD_W), BF16),
            pltpu.VMEM((rows, 1), F32),
            pltpu.VMEM((rows, 1), F32),
            pltpu.VMEM((rows, HEAD_W), F32),
        ],
    )
    return pl.pallas_call(
        functools.partial(_sattn_kernel, pages=pages, dec_seq=dec_seq, out_scale=out_scale),
        grid_spec=grid_spec,
        out_shape=jax.ShapeDtypeStruct((nseq * dec_seq, BRANCH_W), F32),
        compiler_params=pltpu.CompilerParams(dimension_semantics=("parallel", "arbitrary")),
        name="sample_attn",
    )(page_table, jnp.full((1, HEAD_W), lam, F32), subln_w.reshape(1, HEAD_W), q, k_new, v_new,
      *([cache_k] * pages), *([cache_v] * pages))


def _tmlp_kernel(zu_ref, zv_ref, lnw_ref, lnb_ref, wm_ref, bias_ref, oc_ref, *vc_out,
                 mix_rows, period, n_sub):
    u = _gelu(zu_ref[...].astype(F32))
    g = _gelu(zv_ref[...].astype(F32))
    xc = g - jnp.mean(g, axis=-1, keepdims=True)
    vc = xc * lax.rsqrt(jnp.mean(xc * xc, axis=-1, keepdims=True) + EPS) * lnw_ref[...] + lnb_ref[...]
    if vc_out:
        vc_out[0][...] = vc
    row = lax.broadcasted_iota(jnp.int32, (mix_rows, mix_rows), 0)
    col = lax.broadcasted_iota(jnp.int32, (mix_rows, mix_rows), 1)
    sh = _log2(period)
    allow = ((row >> sh) == (col >> sh)) & ((col & (period - 1)) <= (row & (period - 1)))
    for gi in range(N_HEADS):
        sl = slice(gi * HEAD_W, (gi + 1) * HEAD_W)
        wm = jnp.where(allow, wm_ref[gi], 0.0).astype(BF16)
        for sb in range(n_sub):
            rs = slice(sb * mix_rows, (sb + 1) * mix_rows)
            mixed = _dot(wm, vc[rs, sl].astype(BF16)) + bias_ref[:, sl]
            oc_ref[rs, sl] = (u[rs, sl] * mixed).astype(oc_ref.dtype)


def _token_mlp(z, ln_w, ln_b, w_s, b_s, tile_rows, mix_rows, period, emit_v, out_dtype):
    t = z.shape[0]
    reps = mix_rows // period
    wm = jnp.tile(w_s[:, :period, :period], (1, reps, reps))
    bias = jnp.tile(jnp.repeat(b_s[:, :period].T, HEAD_W, axis=1), (reps, 1))
    zspec = lambda cb: pl.BlockSpec((tile_rows, BRANCH_W), lambda i: (i, cb))
    vec = pl.BlockSpec((1, BRANCH_W), lambda i: (0, 0))
    ospec = pl.BlockSpec((tile_rows, BRANCH_W), lambda i: (i, 0))
    out_shape = [jax.ShapeDtypeStruct((t, BRANCH_W), out_dtype)]
    if emit_v:
        out_shape.append(jax.ShapeDtypeStruct((t, BRANCH_W), F32))
    return pl.pallas_call(
        functools.partial(_tmlp_kernel, mix_rows=mix_rows, period=period, n_sub=tile_rows // mix_rows),
        grid=(t // tile_rows,),
        in_specs=[zspec(COL_CU), zspec(COL_CV), vec, vec,
                  pl.BlockSpec((N_HEADS, mix_rows, mix_rows), lambda i: (0, 0, 0)),
                  pl.BlockSpec((mix_rows, BRANCH_W), lambda i: (0, 0))],
        out_specs=[ospec] * len(out_shape),
        out_shape=out_shape,
        compiler_params=pltpu.CompilerParams(dimension_semantics=("parallel",)),
        name="token_mlp",
    )(z, z, ln_w.reshape(1, BRANCH_W), ln_b.reshape(1, BRANCH_W), wm, bias)


def _merge_kernel(oa_ref, ob_ref, oc_ref, zg0_ref, zg1_ref, x_ref, wa_ref, wb_ref, wc_ref, wo_ref,
                  xo_ref):
    pa = _dot(oa_ref[...].astype(BF16), wa_ref[...])
    pb = _dot(ob_ref[...].astype(BF16), wb_ref[...])
    pc = _dot(oc_ref[...].astype(BF16), wc_ref[...])
    zg0 = zg0_ref[...].astype(F32)
    zg1 = zg1_ref[...].astype(F32)
    split = 2 * D_MODEL - GATE_BLOCK_W
    g_a = _sigmoid_gate(zg0[:, :D_MODEL])
    g_b = _sigmoid_gate(jnp.concatenate([zg0[:, D_MODEL:], zg1[:, :split]], axis=1))
    g_c = _sigmoid_gate(zg1[:, split:])
    merged = g_a * pa + g_b * pb + g_c * pc
    xo_ref[...] = x_ref[...] + _dot(merged.astype(BF16), wo_ref[...])


def _merge(oa, ob, oc, z, x, wa, wb, wc, wo, tm):
    t = x.shape[0]
    bspec = pl.BlockSpec((tm, BRANCH_W), lambda i: (i, 0))
    gspec = lambda cb: pl.BlockSpec((tm, GATE_BLOCK_W), lambda i: (i, cb))
    xspec = pl.BlockSpec((tm, D_MODEL), lambda i: (i, 0))
    wspec = pl.BlockSpec((BRANCH_W, D_MODEL), lambda i: (0, 0))
    return pl.pallas_call(
        _merge_kernel,
        grid=(t // tm,),
        in_specs=[bspec, bspec, bspec, gspec(3), gspec(4), xspec, wspec, wspec, wspec,
                  pl.BlockSpec((D_MODEL, D_MODEL), lambda i: (0, 0))],
        out_specs=xspec,
        out_shape=jax.ShapeDtypeStruct((t, D_MODEL), F32),
        compiler_params=pltpu.CompilerParams(dimension_semantics=("parallel",)),
        name="merge",
    )(oa, ob, oc, z, z, x, wa, wb, wc, wo)


def _ffn_kernel(x_ref, nw_ref, wa_ref, wb_ref, wd_ref, cw_ref, cb_ref, cbuf_ref, xo_ref, last_ref,
                h_s, acc_s, abuf, *, tm, shift, tiles_per_seq, pad):
    i = pl.program_id(0)
    j = pl.program_id(1)
    nff = pl.num_programs(1)
    halo = (CONV_W - 1) * shift

    @pl.when(j == 0)
    def _():
        x = x_ref[...]
        ms = jnp.mean(x * x, axis=-1, keepdims=True)
        h_s[...] = (x * lax.rsqrt(ms + EPS) * nw_ref[...]).astype(BF16)
        acc_s[...] = jnp.zeros(acc_s.shape, F32)

    @pl.when(i % tiles_per_seq == 0)
    def _():
        abuf[j, pad - halo:pad, :] = cbuf_ref[0]

    @pl.when(i % tiles_per_seq != 0)
    def _():
        abuf[j, pad - halo:pad, :] = abuf[j, pad + tm - halo:pad + tm, :]

    h = h_s[...]
    tff = wa_ref.shape[1]
    for c0 in range(0, tff, FFN_COL_CHUNK):
        cols = slice(c0, min(c0 + FFN_COL_CHUNK, tff))
        a = _dot(h, wa_ref[:, cols])
        bb = _dot(h, wb_ref[:, cols])
        abuf[j, pad:pad + tm, cols] = a
        a1 = abuf[j, pad - shift:pad - shift + tm, cols]
        a2 = abuf[j, pad - 2 * shift:pad - 2 * shift + tm, cols]
        conv = cb_ref[:, cols] + a2 * cw_ref[0:1, cols] + a1 * cw_ref[1:2, cols] + a * cw_ref[2:3, cols]
        acc_s[...] += _dot((_gelu(conv) * bb).astype(BF16), wd_ref[cols, :])
        last_ref[0, :, cols] = a[tm - halo:, :]

    @pl.when(j == nff - 1)
    def _():
        xo_ref[...] = x_ref[...] + acc_s[...]


def _ffn(x, norm_w, w_up, w_down, conv_w, conv_b, conv_buf, tm, tff, shift, tiles_per_seq):
    t = x.shape[0]
    nff = D_FF // tff
    halo = (CONV_W - 1) * shift
    pad = -(-halo // SUBLANES) * SUBLANES
    n_tiles = t // tm
    wmode = dict(pipeline_mode=pl.Buffered(1)) if nff == 1 else {}
    return pl.pallas_call(
        functools.partial(_ffn_kernel, tm=tm, shift=shift, tiles_per_seq=tiles_per_seq, pad=pad),
        grid=(n_tiles, nff),
        in_specs=[
            pl.BlockSpec((tm, D_MODEL), lambda i, j: (i, 0)),
            pl.BlockSpec((1, D_MODEL), lambda i, j: (0, 0)),
            pl.BlockSpec((D_MODEL, tff), lambda i, j: (0, j), **wmode),
            pl.BlockSpec((D_MODEL, tff), lambda i, j: (0, nff + j), **wmode),
            pl.BlockSpec((tff, D_MODEL), lambda i, j: (j, 0), **wmode),
            pl.BlockSpec((CONV_W, tff), lambda i, j: (0, j)),
            pl.BlockSpec((1, tff), lambda i, j: (0, j)),
            pl.BlockSpec((1, halo, tff), lambda i, j: (i // tiles_per_seq, 0, j)),
        ],
        out_specs=[
            pl.BlockSpec((tm, D_MODEL), lambda i, j: (i, 0)),
            pl.BlockSpec((1, halo, tff), lambda i, j: (i, 0, j)),
        ],
        out_shape=[
            jax.ShapeDtypeStruct((t, D_MODEL), F32),
            jax.ShapeDtypeStruct((n_tiles, halo, D_FF), F32),
        ],
        scratch_shapes=[
            pltpu.VMEM((tm, D_MODEL), BF16),
            pltpu.VMEM((tm, D_MODEL), F32),
            pltpu.VMEM((nff, pad + tm, tff), F32),
        ],
        compiler_params=pltpu.CompilerParams(dimension_semantics=("arbitrary", "arbitrary")),
        name="conv_ffn",
    )(x, norm_w.reshape(1, D_MODEL), w_up, w_up, w_down, conv_w, conv_b.reshape(1, D_FF), conv_buf)


class _Tiles(NamedTuple):
    rows: int
    inproj_rows: int
    inproj_cols: int
    hgrn_chunk: int
    hgrn_chunks_per_step: int
    hgrn_seqs_per_step: int
    attn_block: int


def _tiling(nseq, seq_len):
    t = nseq * seq_len
    chunk = min(A_CHUNK, seq_len)
    return _Tiles(
        rows=min(t, 512),
        inproj_rows=min(t, 1024),
        inproj_cols=6 * MXU_TILE,
        hgrn_chunk=chunk,
        hgrn_chunks_per_step=math.gcd(seq_len // chunk, 16),
        hgrn_seqs_per_step=math.gcd(nseq, 4) if seq_len == chunk else 1,
        attn_block=min(seq_len, 512),
    )


SAMPLE_PAGES_PER_STEP = 32


def _layer(x, nseq, seq_len, layer, pos0, s0, conv_state, past, w, is_sample):
    t = nseq * seq_len
    tiles = _tiling(nseq, seq_len)
    tm = tiles.rows
    act_dtype = F32 if is_sample else BF16
    z, zf = _inproj(x, w["norm_mix_w"], w["w_in"], tiles.inproj_rows, tiles.inproj_cols, act_dtype)

    oa, s_new = _hgrn(z, zf, w["lb"], w["a_norm_w"], s0, nseq, seq_len, tiles.hgrn_chunk,
                      tiles.hgrn_chunks_per_step, tiles.hgrn_seqs_per_step, act_dtype)

    tab_rows = max(seq_len, tm)
    tabs = _rope_tables(pos0, seq_len, tab_rows)
    qkv = _qkv_prep(z, tabs, w["q_norm_w"], w["k_norm_w"], tm, tab_rows // tm, not is_sample)
    lam_init = 0.8 - 0.6 * math.exp(-0.3 * layer)
    lam = (jnp.exp(jnp.sum(w["lambda_q1"] * w["lambda_k1"]))
           - jnp.exp(jnp.sum(w["lambda_q2"] * w["lambda_k2"])) + lam_init)
    if is_sample:
        q, k_rows, v_rows = qkv
        cache_k, cache_v, page_table = past
        ob = _sample_attn(q, k_rows, v_rows, cache_k, cache_v, page_table, layer, lam, w["b_subln_w"],
                          nseq, seq_len, math.gcd(page_table.shape[1], SAMPLE_PAGES_PER_STEP),
                          1.0 - lam_init)
    else:
        q, k_rows, v_rows, k_bf, v_bf = qkv
        ob = _prompt_attn(q, k_bf, v_bf, lam, w["b_subln_w"], nseq, seq_len, tiles.attn_block,
                          tiles.attn_block, 1.0 - lam_init, act_dtype)

    if is_sample:
        oc, vc = _token_mlp(z, w["c_ln_w"], w["c_ln_b"], w["c_w_s"], w["c_b_s"], t, t, seq_len, True,
                            act_dtype)
    else:
        (oc,) = _token_mlp(z, w["c_ln_w"], w["c_ln_b"], w["c_w_s"], w["c_b_s"], tm, C_CHUNK, C_CHUNK,
                           False, act_dtype)
        vc = None

    x_mid = _merge(oa, ob, oc, z, x, w["w_branch_a"], w["w_branch_b"], w["w_branch_c"], w["w_out"], tm)

    if is_sample:
        xt = x_mid.reshape(nseq, seq_len, D_MODEL).transpose(1, 0, 2).reshape(t, D_MODEL)
        cbuf = conv_state.transpose(1, 0, 2).reshape(1, (CONV_W - 1) * nseq, D_FF)
        xo, last = _ffn(xt, w["norm_ffn_w"], w["w_up"], w["w_down"], w["conv_w"], w["conv_b"], cbuf,
                        t, D_FF, nseq, 1)
        x_new = xo.reshape(seq_len, nseq, D_MODEL).transpose(1, 0, 2).reshape(t, D_MODEL)
        new_conv = last.reshape(CONV_W - 1, nseq, D_FF).transpose(1, 0, 2)
    else:
        tiles_per_seq = seq_len // tm
        x_new, last = _ffn(x_mid, w["norm_ffn_w"], w["w_up"], w["w_down"], w["conv_w"], w["conv_b"],
                           conv_state, tm, D_FF, 1, tiles_per_seq)
        new_conv = last.reshape(nseq, tiles_per_seq, CONV_W - 1, D_FF)[:, -1]
    return x_new, s_new, k_rows, v_rows, vc, new_conv


def kernel(x_prompt, x_sample, cache_k, cache_v, page_table, state_hgrn, state_conv,
           norm_mix_w, w_in, lb_logits, a_norm_w, q_norm_w, k_norm_w,
           lambda_q1, lambda_k1, lambda_q2, lambda_k2, b_subln_w,
           c_ln_w, c_ln_b, c_w_s, c_b_s, w_branch_a, w_branch_b, w_branch_c, w_out,
           norm_ffn_w, w_up, conv_w, conv_b, w_down):
    depth = w_in.shape[0]
    n_p, len_p, _ = x_prompt.shape
    n_s, len_s, _ = x_sample.shape
    past_len = page_table.shape[1] * PAGE_SIZE
    p_lb = jax.nn.softmax(lb_logits.astype(F32), axis=0)
    lower_bounds = jnp.cumsum(p_lb, axis=0) - p_lb[0:1]

    xp = x_prompt.reshape(n_p * len_p, D_MODEL)
    xs = x_sample.reshape(n_s * len_s, D_MODEL)
    outs_p, outs_s = [], []
    for l in range(depth):
        w = dict(norm_mix_w=norm_mix_w[l], w_in=w_in[l].astype(BF16), lb=lower_bounds[l],
                 a_norm_w=a_norm_w[l], q_norm_w=q_norm_w[l], k_norm_w=k_norm_w[l],
                 lambda_q1=lambda_q1[l], lambda_k1=lambda_k1[l],
                 lambda_q2=lambda_q2[l], lambda_k2=lambda_k2[l], b_subln_w=b_subln_w[l],
                 c_ln_w=c_ln_w[l], c_ln_b=c_ln_b[l], c_w_s=c_w_s[l], c_b_s=c_b_s[l],
                 w_branch_a=w_branch_a[l].astype(BF16), w_branch_b=w_branch_b[l].astype(BF16),
                 w_branch_c=w_branch_c[l].astype(BF16), w_out=w_out[l].astype(BF16),
                 norm_ffn_w=norm_ffn_w[l], w_up=w_up[l].astype(BF16), conv_w=conv_w[l],
                 conv_b=conv_b[l], w_down=w_down[l].astype(BF16))
        s0_p = jnp.zeros((n_p, N_HEADS, HEAD_W, HEAD_W), F32)
        conv0_p = jnp.zeros((n_p, CONV_W - 1, D_FF), F32)
        xp, *rest_p = _layer(xp, n_p, len_p, l, 0, s0_p, conv0_p, None, w, False)
        outs_p.append(rest_p)
        xs, *rest_s = _layer(xs, n_s, len_s, l, past_len, state_hgrn[l], state_conv[l],
                             (cache_k, cache_v, page_table), w, True)
        outs_s.append(rest_s)

    def stack(outs, idx, shape):
        return jnp.stack([o[idx].reshape(shape) for o in outs])

    kv_p = (n_p, len_p, N_HEADS, HEAD_W)
    kv_s = (n_s, len_s, N_HEADS, HEAD_W)
    return (xp.reshape(n_p, len_p, D_MODEL), xs.reshape(n_s, len_s, D_MODEL),
            stack(outs_p, 1, kv_p), stack(outs_p, 2, kv_p),
            stack(outs_p, 0, (n_p, N_HEADS, HEAD_W, HEAD_W)),
            stack(outs_p, 4, (n_p, CONV_W - 1, D_FF)),
            stack(outs_s, 1, kv_s), stack(outs_s, 2, kv_s),
            stack(outs_s, 0, (n_s, N_HEADS, HEAD_W, HEAD_W)),
            stack(outs_s, 4, (n_s, CONV_W - 1, D_FF)),
            stack(outs_s, 3, (n_s, len_s, BRANCH_W)))
```
